```python
import math
import jax, jax.numpy as jnp
from jax import lax
import numpy as np

D_MODEL = 1024
BATCH = 8
SEQ = 2048
DEPTH = 2
DEC_BATCH = 128
DEC_SEQ = 1
PAST_LEN = 16384
PAGE_SIZE = 128

N_MIXERS = 2
N_META = 16
RMS_EPS = 1e-6
D_RNN = -(-4 * D_MODEL // (3 * 128)) * 128
LRU_BLOCKS = 16
LRU_BW = D_RNN // LRU_BLOCKS
CONV_W = 4
LRU_C = 8.0
S5_GROUP = 16
S5_GROUPS = D_MODEL // S5_GROUP
S5_STATE = 64
CHUNK = 128
D_FF = -(-8 * D_MODEL // (3 * 256)) * 256
N_LRU_LAYERS = (DEPTH + N_MIXERS - 1) // N_MIXERS
N_S5_LAYERS = DEPTH // N_MIXERS

kernel_name = 'hybrid_rglru_s5_meta_decode_step'


def rms_norm(x, g):
    xf = x.astype(jnp.float32)
    y = xf * lax.rsqrt(jnp.mean(xf * xf, axis=-1, keepdims=True) + RMS_EPS)
    return (y * g.astype(jnp.float32)).astype(x.dtype)


def swiglu(x, w_in, w_out):
    gate, up = jnp.split(x @ w_in, 2, axis=-1)
    return (jax.nn.silu(gate) * up) @ w_out


def causal_conv(x, buf, w, b):
    T = x.shape[1]
    xp = jnp.concatenate([buf.astype(x.dtype), x], axis=1)
    y = b + sum(w[k] * xp[:, k:k + T] for k in range(CONV_W))
    return y, xp[:, T:]


def lru_scan(a, b, h0):
    def step(h, ab):
        a_t, b_t = ab
        h = a_t * h + b_t
        return h, h
    h_last, hs = lax.scan(step, h0, (a.swapaxes(0, 1), b.swapaxes(0, 1)))
    return h_last, hs.swapaxes(0, 1)


def rglru_mixer(xn, h0, conv0, w_in, w_conv, b_conv, w_a, b_a, w_x, b_x, lam, w_out):
    B, T, _ = xn.shape
    xb, gb = jnp.split(xn @ w_in, 2, axis=-1)
    xc, conv_new = causal_conv(xb, conv0, w_conv, b_conv)
    xblk = xc.reshape(B, T, LRU_BLOCKS, LRU_BW)
    r = jax.nn.sigmoid(jnp.einsum('bthi,hij->bthj', xblk, w_a).reshape(B, T, D_RNN) + b_a)
    ig = jax.nn.sigmoid(jnp.einsum('bthi,hij->bthj', xblk, w_x).reshape(B, T, D_RNN) + b_x)
    log_a = -LRU_C * r.astype(jnp.float32) * jax.nn.softplus(-lam.astype(jnp.float32))
    a = jnp.exp(log_a)
    b = jnp.sqrt(-jnp.expm1(2.0 * log_a)) * (ig * xc).astype(jnp.float32)
    h_last, hs = lru_scan(a, b, h0.astype(jnp.float32))
    out = (hs.astype(xn.dtype) * jax.nn.gelu(gb)) @ w_out
    return out, h_last, conv_new


def _combine(e1, e2):
    a1, b1 = e1
    a2, b2 = e2
    return a1 * a2, a2 * b1 + b2


def s5_block(h0, u, abar, bbar, c):
    bu = jnp.einsum('btgc,gpc->btgp', u.astype(jnp.complex64), bbar)
    a = jnp.broadcast_to(abar, bu.shape)
    a_cum, h_loc = lax.associative_scan(_combine, (a, bu), axis=1)
    h = h_loc + a_cum * h0[:, None]
    y = jnp.real(jnp.einsum('gcp,btgp->btgc', c, h))
    return h[:, -1], y


def s5_mixer(xn, h0_re, h0_im, lead, lam_re, lam_im, log_dt, b_re, b_im, c_re, c_im, d_skip, w_out):
    B, T, D = xn.shape
    f32 = jnp.float32
    A = lax.complex(lam_re.astype(f32), lam_im.astype(f32))
    dt = jnp.exp(log_dt.astype(f32))[:, None]
    abar = jnp.exp(dt * A)
    bbar = ((abar - 1.0) / A)[:, :, None] * lax.complex(b_re.astype(f32), b_im.astype(f32))
    c = lax.complex(c_re.astype(f32), c_im.astype(f32))
    u = xn.astype(f32).reshape(B, T, S5_GROUPS, S5_GROUP)
    h0 = lax.complex(h0_re.astype(f32), h0_im.astype(f32))
    h, y_lead = s5_block(h0, u[:, :lead], abar, bbar, c)
    ys = [y_lead]
    n_rest = (T - lead) // CHUNK
    if n_rest > 0:
        u_rest = u[:, lead:].reshape(B, n_rest, CHUNK, S5_GROUPS, S5_GROUP).swapaxes(0, 1)
        h, y_rest = lax.scan(lambda hc, uc: s5_block(hc, uc, abar, bbar, c), h, u_rest)
        ys.append(y_rest.swapaxes(0, 1).reshape(B, T - lead, S5_GROUPS, S5_GROUP))
    y = jnp.concatenate(ys, axis=1).reshape(B, T, D) + d_skip.astype(f32) * xn.astype(f32)
    z = jax.nn.gelu(y).astype(xn.dtype)
    za, zb = jnp.split(z @ w_out, 2, axis=-1)
    return za * jax.nn.sigmoid(zb), jnp.real(h), jnp.imag(h)


def setup_inputs(seed: int = 0) -> dict:
    key = jax.random.key(seed)
    ks = jax.random.split(key, 32)
    nrm = jax.random.normal
    f32 = jnp.float32
    L1, L2 = N_LRU_LAYERS, N_S5_LAYERS
    a0 = jax.random.uniform(ks[14], (L1, D_RNN), f32, 0.9, 0.999)
    n_idx = jnp.arange(S5_STATE, dtype=f32)
    return {
        'x_prompt': nrm(ks[0], (BATCH, SEQ, D_MODEL), f32),
        'x_sample': nrm(ks[1], (DEC_BATCH, DEC_SEQ, D_MODEL), f32),
        'state_lru_h': 0.5 * nrm(ks[2], (L1, DEC_BATCH, D_RNN), f32),
        'state_lru_conv': nrm(ks[3], (L1, DEC_BATCH, CONV_W - 1, D_RNN), f32),
        'state_s5_re': 0.5 * nrm(ks[4], (L2, DEC_BATCH, S5_GROUPS, S5_STATE), f32),
        'state_s5_im': 0.5 * nrm(ks[5], (L2, DEC_BATCH, S5_GROUPS, S5_STATE), f32),
        'meta_tokens': nrm(ks[6], (N_META, D_MODEL), f32),
        'norm_gains': 1.0 + 0.1 * nrm(ks[7], (DEPTH, 4, D_MODEL), f32),
        'lru_w_in': nrm(ks[8], (L1, D_MODEL, 2 * D_RNN), f32) * D_MODEL ** -0.5,
        'lru_w_conv': nrm(ks[9], (L1, CONV_W, D_RNN), f32) * CONV_W ** -0.5,
        'lru_b_conv': 0.01 * nrm(ks[10], (L1, D_RNN), f32),
        'lru_w_a': nrm(ks[11], (L1, LRU_BLOCKS, LRU_BW, LRU_BW), f32) * LRU_BW ** -0.5,
        'lru_b_a': 0.1 * nrm(ks[12], (L1, D_RNN), f32),
        'lru_w_x': nrm(ks[13], (L1, LRU_BLOCKS, LRU_BW, LRU_BW), f32) * LRU_BW ** -0.5,
        'lru_b_x': 0.1 * nrm(ks[15], (L1, D_RNN), f32),
        'lru_lambda': jnp.log(a0) - jnp.log1p(-a0),
        'lru_w_out': nrm(ks[16], (L1, D_RNN, D_MODEL), f32) * D_RNN ** -0.5,
        's5_lambda_re': -0.5 + 0.01 * nrm(ks[17], (L2, S5_GROUPS, S5_STATE), f32),
        's5_lambda_im': math.pi * n_idx + 0.01 * nrm(ks[18], (L2, S5_GROUPS, S5_STATE), f32),
        's5_log_dt': jax.random.uniform(ks[19], (L2, S5_GROUPS), f32, math.log(0.001), math.log(0.1)),
        's5_b_re': nrm(ks[20], (L2, S5_GROUPS, S5_STATE, S5_GROUP), f32) * (2 * S5_GROUP) ** -0.5,
        's5_b_im': nrm(ks[21], (L2, S5_GROUPS, S5_STATE, S5_GROUP), f32) * (2 * S5_GROUP) ** -0.5,
        's5_c_re': nrm(ks[22], (L2, S5_GROUPS, S5_GROUP, S5_STATE), f32) * S5_STATE ** -0.5,
        's5_c_im': nrm(ks[23], (L2, S5_GROUPS, S5_GROUP, S5_STATE), f32) * S5_STATE ** -0.5,
        's5_d': nrm(ks[24], (L2, D_MODEL), f32),
        's5_w_out': nrm(ks[25], (L2, D_MODEL, 2 * D_MODEL), f32) * D_MODEL ** -0.5,
        'ffn_w_in': nrm(ks[26], (DEPTH, D_MODEL, 2 * D_FF), f32) * D_MODEL ** -0.5,
        'ffn_w_out': nrm(ks[27], (DEPTH, D_FF, D_MODEL), f32) * D_FF ** -0.5,
    }


def reference(x_prompt, x_sample, state_lru_h, state_lru_conv, state_s5_re, state_s5_im,
              meta_tokens, norm_gains, lru_w_in, lru_w_conv, lru_b_conv, lru_w_a, lru_b_a,
              lru_w_x, lru_b_x, lru_lambda, lru_w_out, s5_lambda_re, s5_lambda_im, s5_log_dt,
              s5_b_re, s5_b_im, s5_c_re, s5_c_im, s5_d, s5_w_out, ffn_w_in, ffn_w_out):
    Bp = x_prompt.shape[0]
    x_p = jnp.concatenate(
        [jnp.broadcast_to(meta_tokens.astype(x_prompt.dtype)[None], (Bp, N_META, D_MODEL)), x_prompt], axis=1)
    x_s = x_sample
    lru_h_p, lru_cv_p, s5_re_p, s5_im_p = [], [], [], []
    lru_h_s, lru_cv_s, s5_re_s, s5_im_s = [], [], [], []
    for i in range(DEPTH):
        g = norm_gains[i]
        j = i // N_MIXERS
        xn_p = rms_norm(x_p, g[0])
        xn_s = rms_norm(x_s, g[0])
        if i % N_MIXERS == 0:
            prm = (lru_w_in[j], lru_w_conv[j], lru_b_conv[j], lru_w_a[j], lru_b_a[j],
                   lru_w_x[j], lru_b_x[j], lru_lambda[j], lru_w_out[j])
            m_p, h_p, cv_p = rglru_mixer(xn_p, jnp.zeros((Bp, D_RNN), jnp.float32),
                                         jnp.zeros((Bp, CONV_W - 1, D_RNN), x_p.dtype), *prm)
            m_s, h_s, cv_s = rglru_mixer(xn_s, state_lru_h[j], state_lru_conv[j], *prm)
            lru_h_p.append(h_p); lru_cv_p.append(cv_p)
            lru_h_s.append(h_s); lru_cv_s.append(cv_s)
        else:
            prm = (s5_lambda_re[j], s5_lambda_im[j], s5_log_dt[j], s5_b_re[j], s5_b_im[j],
                   s5_c_re[j], s5_c_im[j], s5_d[j], s5_w_out[j])
            zeros_state = jnp.zeros((Bp, S5_GROUPS, S5_STATE), jnp.float32)
            m_p, re_p, im_p = s5_mixer(xn_p, zeros_state, zeros_state, N_META, *prm)
            m_s, re_s, im_s = s5_mixer(xn_s, state_s5_re[j], state_s5_im[j], x_s.shape[1], *prm)
            s5_re_p.append(re_p); s5_im_p.append(im_p)
            s5_re_s.append(re_s); s5_im_s.append(im_s)
        x_p = x_p + rms_norm(m_p, g[1])
        x_s = x_s + rms_norm(m_s, g[1])
        x_p = x_p + rms_norm(swiglu(rms_norm(x_p, g[2]), ffn_w_in[i], ffn_w_out[i]), g[3])
        x_s = x_s + rms_norm(swiglu(rms_norm(x_s, g[2]), ffn_w_in[i], ffn_w_out[i]), g[3])
    y_prompt = x_p[:, N_META:]
    y_sample = x_s
    new_lru_h_prompt = jnp.stack(lru_h_p)
    new_lru_conv_prompt = jnp.stack(lru_cv_p)
    new_s5_re_prompt = jnp.stack(s5_re_p)
    new_s5_im_prompt = jnp.stack(s5_im_p)
    new_lru_h_sample = jnp.stack(lru_h_s)
    new_lru_conv_sample = jnp.stack(lru_cv_s)
    new_s5_re_sample = jnp.stack(s5_re_s)
    new_s5_im_sample = jnp.stack(s5_im_s)
    return (y_prompt, y_sample, new_lru_h_prompt, new_lru_conv_prompt, new_s5_re_prompt, new_s5_im_prompt,
            new_lru_h_sample, new_lru_conv_sample, new_s5_re_sample, new_s5_im_sample)
```

```python
import functools
import math

import jax
import jax.numpy as jnp
from jax import lax
from jax.experimental import pallas as pl
from jax.experimental.pallas import tpu as pltpu

D_MODEL = 1024
N_META = 16
RMS_EPS = 1e-6
D_RNN = 1408
LRU_BLOCKS = 16
LRU_BW = D_RNN // LRU_BLOCKS
CONV_W = 4
LRU_C = 8.0
S5_GROUP = 16
S5_GROUPS = D_MODEL // S5_GROUP
S5_STATE = 64
D_FF = 2816

LANES = 128
MXU_DIM = 256
VMEM_LIMIT_BYTES = 56 * 1024 * 1024

S5_KTILES = D_MODEL // MXU_DIM
S5_TILE_STATES = (MXU_DIM // S5_GROUP) * S5_STATE
S5_SCAN_W = 512

BF16 = jnp.bfloat16
F32 = jnp.float32


def _rms(x, g):
    return x * lax.rsqrt(jnp.mean(x * x, axis=-1, keepdims=True) + RMS_EPS) * g


def _const_spec(shape):
    nd = len(shape)
    return pl.BlockSpec(shape, lambda i: (0,) * nd, pipeline_mode=pl.Buffered(1))


def _row_spec(rows, cols):
    return pl.BlockSpec((rows, cols), lambda i: (i, 0))


def _params(sem):
    return pltpu.CompilerParams(dimension_semantics=(sem,), vmem_limit_bytes=VMEM_LIMIT_BYTES)


def _gate_bands():
    bands = []
    for n0 in range(0, D_RNN, MXU_DIM):
        n1 = min(n0 + MXU_DIM, D_RNN)
        k0 = (n0 // LRU_BW) * LRU_BW
        k1 = -(-n1 // LRU_BW) * LRU_BW
        k0 = (k0 // LANES) * LANES
        k1 = min(D_RNN, -(-k1 // LANES) * LANES)
        bands.append((n0, n1, k0, k1))
    return tuple(bands)


GATE_BANDS = _gate_bands()


def _banded_gate(xc_bf16, w_ref):
    outs = [jnp.dot(xc_bf16[:, k0:k1], w_ref[k0:k1, n0:n1], preferred_element_type=F32)
            for (n0, n1, k0, k1) in GATE_BANDS]
    return jnp.concatenate(outs, axis=1)


def _lru_kernel(x_ref, h0_ref, conv0_ref, g0_ref, g1_ref, w_in_ref, w_conv_ref, b_conv_ref,
                wa_ref, ba_ref, wx_ref, bx_ref, lam_ref, w_out_ref,
                out_ref, hlast_ref, convnew_ref,
                ext_ref, h_ref, a_ref, b_ref, *, bsz, steps):
    rows = bsz * steps
    halo = (CONV_W - 1) * bsz

    @pl.when(pl.program_id(0) == 0)
    def _():
        ext_ref[0:halo, :] = conv0_ref[...]
        h_ref[...] = h0_ref[...]

    x = x_ref[...]
    xn = _rms(x, g0_ref[...]).astype(BF16)
    proj = jnp.dot(xn, w_in_ref[...], preferred_element_type=F32)
    gb = proj[:, D_RNN:]
    ext_ref[halo:halo + rows, :] = proj[:, :D_RNN]

    xc = b_conv_ref[...] + w_conv_ref[0:1, :] * ext_ref[0:rows, :]
    for k in range(1, CONV_W):
        xc = xc + w_conv_ref[k:k + 1, :] * ext_ref[k * bsz:k * bsz + rows, :]
    new_halo = ext_ref[rows:rows + halo, :]
    ext_ref[0:halo, :] = new_halo
    convnew_ref[...] = new_halo

    xcb = xc.astype(BF16)
    r = jax.nn.sigmoid(_banded_gate(xcb, wa_ref) + ba_ref[...])
    ig = jax.nn.sigmoid(_banded_gate(xcb, wx_ref) + bx_ref[...])
    neg_lam = -lam_ref[...]
    softplus = jnp.maximum(neg_lam, 0.0) + jnp.log1p(jnp.exp(-jnp.abs(neg_lam)))
    log_a = (-LRU_C) * r * softplus
    a = jnp.exp(log_a)
    a_ref[...] = a
    b_ref[...] = jnp.sqrt(-jnp.tanh(log_a) * (a * a + 1.0)) * (ig * xc)

    def step(t, h):
        rs = pl.ds(pl.multiple_of(t * bsz, bsz), bsz)
        h = a_ref[rs, :] * h + b_ref[rs, :]
        b_ref[rs, :] = h
        return h

    h = lax.fori_loop(0, steps, step, h_ref[...])
    h_ref[...] = h
    hlast_ref[...] = h

    y = (b_ref[...] * jax.nn.gelu(gb)).astype(BF16)
    m = jnp.dot(y, w_out_ref[...], preferred_element_type=F32)
    out_ref[...] = x + _rms(m, g1_ref[...])


def _lru_layer(x, h0, conv0, g0, g1, w_in, w_conv, b_conv, wa, ba, wx, bx, lam, w_out, *, bsz, steps):
    n_rows = x.shape[0]
    rows = bsz * steps
    halo = (CONV_W - 1) * bsz
    assert n_rows % rows == 0
    kern = functools.partial(_lru_kernel, bsz=bsz, steps=steps)
    return pl.pallas_call(
        kern,
        grid=(n_rows // rows,),
        in_specs=[
            _row_spec(rows, D_MODEL),
            _const_spec((bsz, D_RNN)), _const_spec((halo, D_RNN)),
            _const_spec((1, D_MODEL)), _const_spec((1, D_MODEL)),
            _const_spec((D_MODEL, 2 * D_RNN)),
            _const_spec((CONV_W, D_RNN)), _const_spec((1, D_RNN)),
            _const_spec((D_RNN, D_RNN)), _const_spec((1, D_RNN)),
            _const_spec((D_RNN, D_RNN)), _const_spec((1, D_RNN)),
            _const_spec((1, D_RNN)),
            _const_spec((D_RNN, D_MODEL)),
        ],
        out_specs=[
            _row_spec(rows, D_MODEL),
            pl.BlockSpec((bsz, D_RNN), lambda i: (0, 0)),
            pl.BlockSpec((halo, D_RNN), lambda i: (0, 0)),
        ],
        out_shape=[
            jax.ShapeDtypeStruct((n_rows, D_MODEL), F32),
            jax.ShapeDtypeStruct((bsz, D_RNN), F32),
            jax.ShapeDtypeStruct((halo, D_RNN), F32),
        ],
        scratch_shapes=[
            pltpu.VMEM((halo + rows, D_RNN), F32),
            pltpu.VMEM((bsz, D_RNN), F32),
            pltpu.VMEM((rows, D_RNN), F32),
            pltpu.VMEM((rows, D_RNN), F32),
        ],
        compiler_params=_params("arbitrary"),
        name="lru_layer",
    )(x, h0, conv0, g0, g1, w_in, w_conv, b_conv, wa, ba, wx, bx, lam, w_out)


FFN_CHUNK = D_FF // 2


def _ffn_kernel(x_ref, g2_ref, g3_ref, w_in_ref, w_out_ref, out_ref, act_ref):
    x = x_ref[...]
    xn = _rms(x, g2_ref[...]).astype(BF16)
    for c0 in range(0, D_FF, FFN_CHUNK):
        gate = jnp.dot(xn, w_in_ref[:, c0:c0 + FFN_CHUNK], preferred_element_type=F32)
        up = jnp.dot(xn, w_in_ref[:, D_FF + c0:D_FF + c0 + FFN_CHUNK], preferred_element_type=F32)
        act_ref[:, c0:c0 + FFN_CHUNK] = (jax.nn.silu(gate) * up).astype(BF16)
    m = jnp.dot(act_ref[...], w_out_ref[...], preferred_element_type=F32)
    out_ref[...] = x + _rms(m, g3_ref[...])


def _ffn_layer(x, g2, g3, w_in, w_out, *, rows):
    n_rows = x.shape[0]
    assert n_rows % rows == 0
    return pl.pallas_call(
        _ffn_kernel,
        grid=(n_rows // rows,),
        in_specs=[
            _row_spec(rows, D_MODEL),
            _const_spec((1, D_MODEL)), _const_spec((1, D_MODEL)),
            _const_spec((D_MODEL, 2 * D_FF)),
            _const_spec((D_FF, D_MODEL)),
        ],
        out_specs=_row_spec(rows, D_MODEL),
        out_shape=jax.ShapeDtypeStruct((n_rows, D_MODEL), F32),
        scratch_shapes=[pltpu.VMEM((rows, D_FF), BF16)],
        compiler_params=_params("parallel"),
        name="ffn_layer",
    )(x, g2, g3, w_in, w_out)


def _s5_disc_kernel(lre_ref, lim_ref, logdt_ref, bre_ref, bim_ref,
                    are_ref, aim_ref, bbre_ref, bbim_ref):
    lre = lre_ref[...]
    lim = lim_ref[...]
    dt = jnp.exp(logdt_ref[...])
    mag = jnp.exp(dt * lre)
    are = mag * jnp.cos(dt * lim)
    aim = mag * jnp.sin(dt * lim)
    are_ref[...] = are
    aim_ref[...] = aim
    nre = are - 1.0
    den = lre * lre + lim * lim
    cre = (nre * lre + aim * lim) / den
    cim = (aim * lre - nre * lim) / den
    shape3 = (S5_GROUPS, S5_GROUP, S5_STATE)
    cre3 = jnp.broadcast_to(cre[:, None, :], shape3).reshape(S5_GROUPS * S5_GROUP, S5_STATE)
    cim3 = jnp.broadcast_to(cim[:, None, :], shape3).reshape(S5_GROUPS * S5_GROUP, S5_STATE)
    bre = bre_ref[...]
    bim = bim_ref[...]
    bbre_ref[...] = cre3 * bre - cim3 * bim
    bbim_ref[...] = cre3 * bim + cim3 * bre


def _s5_discretize(lam_re, lam_im, log_dt, bt_re, bt_im):
    gp = jax.ShapeDtypeStruct((S5_GROUPS, S5_STATE), F32)
    gcp = jax.ShapeDtypeStruct((S5_GROUPS * S5_GROUP, S5_STATE), F32)
    return pl.pallas_call(
        _s5_disc_kernel,
        out_shape=[gp, gp, gcp, gcp],
        name="s5_discretize",
    )(lam_re, lam_im, log_dt, bt_re, bt_im)


def _s5_kernel(x_ref, hre0_ref, him0_ref, g0_ref, g1_ref, are_ref, aim_ref, wb_ref, wc_ref,
               dskip_ref, w_out_ref,
               out_ref, hre_out_ref, him_out_ref,
               hre_ref, him_ref, bu_ref, y_ref, *, bsz, steps):
    rows = bsz * steps
    ts = S5_TILE_STATES

    @pl.when(pl.program_id(0) == 0)
    def _():
        hre_ref[...] = hre0_ref[...]
        him_ref[...] = him0_ref[...]

    x = x_ref[...]
    u = _rms(x, g0_ref[...])
    ub = u.astype(BF16)

    for k in range(S5_KTILES):
        bu_ref[...] = jnp.dot(ub[:, k * MXU_DIM:(k + 1) * MXU_DIM], wb_ref[k],
                              preferred_element_type=F32)
        for c0 in range(0, ts, S5_SCAN_W):
            lanes = slice(k * ts + c0, k * ts + c0 + S5_SCAN_W)
            re_cols = slice(c0, c0 + S5_SCAN_W)
            im_cols = slice(ts + c0, ts + c0 + S5_SCAN_W)
            ar = jnp.broadcast_to(are_ref[:, lanes], (bsz, S5_SCAN_W))
            ai = jnp.broadcast_to(aim_ref[:, lanes], (bsz, S5_SCAN_W))

            def step(t, carry, re_cols=re_cols, im_cols=im_cols, ar=ar, ai=ai):
                hr, hi = carry
                rs = pl.ds(pl.multiple_of(t * bsz, bsz), bsz)
                nr = (ar * hr - ai * hi) + bu_ref[rs, re_cols]
                ni = (ar * hi + ai * hr) + bu_ref[rs, im_cols]
                bu_ref[rs, re_cols] = nr
                bu_ref[rs, im_cols] = ni
                return nr, ni

            hr, hi = lax.fori_loop(0, steps, step, (hre_ref[:, lanes], him_ref[:, lanes]))
            hre_ref[:, lanes] = hr
            him_ref[:, lanes] = hi
        y_ref[:, k * MXU_DIM:(k + 1) * MXU_DIM] = jnp.dot(
            bu_ref[...].astype(BF16), wc_ref[k], preferred_element_type=F32)

    hre_out_ref[...] = hre_ref[...]
    him_out_ref[...] = him_ref[...]

    y = y_ref[...] + dskip_ref[...] * u
    z = jax.nn.gelu(y).astype(BF16)
    zz = jnp.dot(z, w_out_ref[...], preferred_element_type=F32)
    m = zz[:, :D_MODEL] * jax.nn.sigmoid(zz[:, D_MODEL:])
    out_ref[...] = x + _rms(m, g1_ref[...])


def _s5_layer(x, hre0, him0, g0, g1, are, aim, wb, wc, dskip, w_out, *, bsz, steps):
    n_rows = x.shape[0]
    rows = bsz * steps
    n_state = S5_GROUPS * S5_STATE
    assert n_rows % rows == 0
    kern = functools.partial(_s5_kernel, bsz=bsz, steps=steps)
    state = jax.ShapeDtypeStruct((bsz, n_state), F32)
    return pl.pallas_call(
        kern,
        grid=(n_rows // rows,),
        in_specs=[
            _row_spec(rows, D_MODEL),
            _const_spec((bsz, n_state)), _const_spec((bsz, n_state)),
            _const_spec((1, D_MODEL)), _const_spec((1, D_MODEL)),
            _const_spec((1, n_state)), _const_spec((1, n_state)),
            _const_spec((S5_KTILES, MXU_DIM, 2 * S5_TILE_STATES)),
            _const_spec((S5_KTILES, 2 * S5_TILE_STATES, MXU_DIM)),
            _const_spec((1, D_MODEL)),
            _const_spec((D_MODEL, 2 * D_MODEL)),
        ],
        out_specs=[
            _row_spec(rows, D_MODEL),
            pl.BlockSpec((bsz, n_state), lambda i: (0, 0)),
            pl.BlockSpec((bsz, n_state), lambda i: (0, 0)),
        ],
        out_shape=[jax.ShapeDtypeStruct((n_rows, D_MODEL), F32), state, state],
        scratch_shapes=[
            pltpu.VMEM((bsz, n_state), F32),
            pltpu.VMEM((bsz, n_state), F32),
            pltpu.VMEM((rows, 2 * S5_TILE_STATES), F32),
            pltpu.VMEM((rows, D_MODEL), F32),
        ],
        compiler_params=_params("arbitrary"),
        name="s5_layer",
    )(x, hre0, him0, g0, g1, are, aim, wb, wc, dskip, w_out)


def _block_diag(blocks):
    n, r, c = blocks.shape
    eye = jnp.eye(n, dtype=blocks.dtype)
    return (blocks[:, :, None, :] * eye[:, None, :, None]).reshape(n * r, n * c)


def _tile_block_diag(blocks):
    per = S5_GROUPS // S5_KTILES
    g, r, c = blocks.shape
    return jax.vmap(_block_diag)(blocks.reshape(S5_KTILES, per, r, c))


def _row(v):
    return v.reshape(1, -1)


def kernel(x_prompt, x_sample, state_lru_h, state_lru_conv, state_s5_re, state_s5_im, meta_tokens, norm_gains, lru_w_in, lru_w_conv, lru_b_conv, lru_w_a, lru_b_a, lru_w_x, lru_b_x, lru_lambda, lru_w_out, s5_lambda_re, s5_lambda_im, s5_log_dt, s5_b_re, s5_b_im, s5_c_re, s5_c_im, s5_d, s5_w_out, ffn_w_in, ffn_w_out):
    bp, seq, _ = x_prompt.shape
    bs = x_sample.shape[0]
    tp = seq + N_META
    n_state = S5_GROUPS * S5_STATE

    lru_in = lru_w_in[0].astype(BF16)
    wa = _block_diag(lru_w_a[0]).astype(BF16)
    wx = _block_diag(lru_w_x[0]).astype(BF16)
    lru_out = lru_w_out[0].astype(BF16)
    ffn_in = [ffn_w_in[i].astype(BF16) for i in range(2)]
    ffn_out = [ffn_w_out[i].astype(BF16) for i in range(2)]
    s5_out = s5_w_out[0].astype(BF16)

    bt_re = s5_b_re[0].transpose(0, 2, 1).reshape(S5_GROUPS * S5_GROUP, S5_STATE)
    bt_im = s5_b_im[0].transpose(0, 2, 1).reshape(S5_GROUPS * S5_GROUP, S5_STATE)
    are, aim, bb_re, bb_im = _s5_discretize(
        s5_lambda_re[0], s5_lambda_im[0], s5_log_dt[0].reshape(S5_GROUPS, 1), bt_re, bt_im)
    shape_b = (S5_GROUPS, S5_GROUP, S5_STATE)
    wb = jnp.concatenate([_tile_block_diag(bb_re.reshape(shape_b)),
                          _tile_block_diag(bb_im.reshape(shape_b))], axis=2).astype(BF16)
    ct_re = s5_c_re[0].transpose(0, 2, 1)
    ct_im = s5_c_im[0].transpose(0, 2, 1)
    wc = jnp.concatenate([_tile_block_diag(ct_re), _tile_block_diag(-ct_im)], axis=1).astype(BF16)
    are = are.reshape(1, n_state)
    aim = aim.reshape(1, n_state)

    g = norm_gains

    def run(x_rows, h0, conv0, hre0, him0, bsz, lru_steps, s5_steps, ffn_rows):
        x1, h_last, conv_new = _lru_layer(
            x_rows, h0, conv0, _row(g[0, 0]), _row(g[0, 1]), lru_in, lru_w_conv[0], _row(lru_b_conv[0]),
            wa, _row(lru_b_a[0]), wx, _row(lru_b_x[0]), _row(lru_lambda[0]), lru_out,
            bsz=bsz, steps=lru_steps)
        x2 = _ffn_layer(x1, _row(g[0, 2]), _row(g[0, 3]), ffn_in[0], ffn_out[0], rows=ffn_rows)
        x3, hre, him = _s5_layer(
            x2, hre0, him0, _row(g[1, 0]), _row(g[1, 1]), are, aim, wb, wc, _row(s5_d[0]), s5_out,
            bsz=bsz, steps=s5_steps)
        x4 = _ffn_layer(x3, _row(g[1, 2]), _row(g[1, 3]), ffn_in[1], ffn_out[1], rows=ffn_rows)
        return x4, h_last, conv_new, hre, him

    meta = jnp.broadcast_to(meta_tokens.astype(x_prompt.dtype)[None], (bp, N_META, D_MODEL))
    xp = jnp.concatenate([meta, x_prompt], axis=1).transpose(1, 0, 2).reshape(tp * bp, D_MODEL)
    zeros_h = jnp.zeros((bp, D_RNN), F32)
    zeros_c = jnp.zeros(((CONV_W - 1) * bp, D_RNN), F32)
    zeros_s = jnp.zeros((bp, n_state), F32)
    yp, hp, cp, rep, imp = run(xp, zeros_h, zeros_c, zeros_s, zeros_s, bp, 48, 86, 688)
    y_prompt = yp.reshape(tp, bp, D_MODEL)[N_META:].transpose(1, 0, 2)

    xs = x_sample.reshape(bs, D_MODEL)
    conv0_s = state_lru_conv[0].transpose(1, 0, 2).reshape((CONV_W - 1) * bs, D_RNN)
    ys, hs, cs, res, ims = run(xs, state_lru_h[0], conv0_s,
                               state_s5_re[0].reshape(bs, n_state), state_s5_im[0].reshape(bs, n_state),
                               bs, 1, 1, bs)
    y_sample = ys.reshape(bs, 1, D_MODEL)

    def conv_out(c, b):
        return c.reshape(CONV_W - 1, b, D_RNN).transpose(1, 0, 2)[None]

    def s5_out_state(s, b):
        return s.reshape(1, b, S5_GROUPS, S5_STATE)

    return (y_prompt, y_sample,
            hp[None], conv_out(cp, bp), s5_out_state(rep, bp), s5_out_state(imp, bp),
            hs[None], conv_out(cs, bs), s5_out_state(res, bs), s5_out_state(ims, bs))
```

```python
import functools
import math

import jax
import jax.numpy as jnp
from jax import lax
from jax.experimental import pallas as pl
from jax.experimental.pallas import tpu as pltpu

D_MODEL = 1024
N_META = 16
RMS_EPS = 1e-6
D_RNN = 1408
LRU_BLOCKS = 16
LRU_BW = D_RNN // LRU_BLOCKS
CONV_W = 4
LRU_C = 8.0
S5_GROUP = 16
S5_GROUPS = D_MODEL // S5_GROUP
S5_STATE = 64
D_FF = 2816

LANES = 128
MXU_DIM = 256
VMEM_LIMIT_BYTES = 56 * 1024 * 1024

S5_KTILES = D_MODEL // MXU_DIM
S5_TILE_STATES = (MXU_DIM // S5_GROUP) * S5_STATE
S5_SCAN_W = 512
PROMPT_STEPS = 64

BF16 = jnp.bfloat16
F32 = jnp.float32


def _rms(x, g):
    return x * lax.rsqrt(jnp.mean(x * x, axis=-1, keepdims=True) + RMS_EPS) * g


def _const_spec(shape):
    nd = len(shape)
    return pl.BlockSpec(shape, lambda i: (0,) * nd, pipeline_mode=pl.Buffered(1))


def _row_spec(rows, cols):
    return pl.BlockSpec((rows, cols), lambda i: (i, 0))


def _params(sem):
    return pltpu.CompilerParams(dimension_semantics=(sem,), vmem_limit_bytes=VMEM_LIMIT_BYTES)


def _gate_bands():
    bands = []
    for n0 in range(0, D_RNN, MXU_DIM):
        n1 = min(n0 + MXU_DIM, D_RNN)
        k0 = (n0 // LRU_BW) * LRU_BW
        k1 = -(-n1 // LRU_BW) * LRU_BW
        k0 = (k0 // LANES) * LANES
        k1 = min(D_RNN, -(-k1 // LANES) * LANES)
        bands.append((n0, n1, k0, k1))
    return tuple(bands)


GATE_BANDS = _gate_bands()


def _banded_gate(xc_bf16, w_ref):
    outs = [jnp.dot(xc_bf16[:, k0:k1], w_ref[k0:k1, n0:n1], preferred_element_type=F32)
            for (n0, n1, k0, k1) in GATE_BANDS]
    return jnp.concatenate(outs, axis=1)


def _lru_kernel(x_ref, h0_ref, conv0_ref, g0_ref, g1_ref, w_in_ref, w_conv_ref, b_conv_ref,
                wa_ref, ba_ref, wx_ref, bx_ref, lam_ref, w_out_ref,
                out_ref, hlast_ref, convnew_ref,
                ext_ref, h_ref, a_ref, b_ref, *, bsz, steps, batch_major_in):
    rows = bsz * steps
    halo = (CONV_W - 1) * bsz

    @pl.when(pl.program_id(0) == 0)
    def _():
        ext_ref[0:halo, :] = conv0_ref[...]
        h_ref[...] = h0_ref[...]

    if batch_major_in:
        x = jnp.swapaxes(x_ref[...], 0, 1).reshape(rows, D_MODEL)
    else:
        x = x_ref[...]
    xn = _rms(x, g0_ref[...]).astype(BF16)
    proj = jnp.dot(xn, w_in_ref[...], preferred_element_type=F32)
    gb = proj[:, D_RNN:]
    ext_ref[halo:halo + rows, :] = proj[:, :D_RNN]

    xc = b_conv_ref[...] + w_conv_ref[0:1, :] * ext_ref[0:rows, :]
    for k in range(1, CONV_W):
        xc = xc + w_conv_ref[k:k + 1, :] * ext_ref[k * bsz:k * bsz + rows, :]
    new_halo = ext_ref[rows:rows + halo, :]
    ext_ref[0:halo, :] = new_halo
    convnew_ref[...] = new_halo

    xcb = xc.astype(BF16)
    r = jax.nn.sigmoid(_banded_gate(xcb, wa_ref) + ba_ref[...])
    ig = jax.nn.sigmoid(_banded_gate(xcb, wx_ref) + bx_ref[...])
    neg_lam = -lam_ref[...]
    softplus = jnp.maximum(neg_lam, 0.0) + jnp.log1p(jnp.exp(-jnp.abs(neg_lam)))
    log_a = (-LRU_C) * r * softplus
    a = jnp.exp(log_a)
    a_ref[...] = a
    b_ref[...] = jnp.sqrt(-jnp.tanh(log_a) * (a * a + 1.0)) * (ig * xc)

    def step(t, h):
        rs = pl.ds(pl.multiple_of(t * bsz, bsz), bsz)
        h = a_ref[rs, :] * h + b_ref[rs, :]
        b_ref[rs, :] = h
        return h

    h = lax.fori_loop(0, steps, step, h_ref[...])
    h_ref[...] = h
    hlast_ref[...] = h

    y = (b_ref[...] * jax.nn.gelu(gb)).astype(BF16)
    m = jnp.dot(y, w_out_ref[...], preferred_element_type=F32)
    out_ref[...] = x + _rms(m, g1_ref[...])


def _lru_layer(x, h0, conv0, g0, g1, w_in, w_conv, b_conv, wa, ba, wx, bx, lam, w_out, *, bsz, steps):
    batch_major_in = x.ndim == 3
    rows = bsz * steps
    halo = (CONV_W - 1) * bsz
    if batch_major_in:
        assert x.shape[0] == bsz and x.shape[1] % steps == 0
        n_rows = bsz * x.shape[1]
        x_spec = pl.BlockSpec((bsz, steps, D_MODEL), lambda i: (0, i, 0))
    else:
        n_rows = x.shape[0]
        assert n_rows % rows == 0
        x_spec = _row_spec(rows, D_MODEL)
    kern = functools.partial(_lru_kernel, bsz=bsz, steps=steps, batch_major_in=batch_major_in)
    return pl.pallas_call(
        kern,
        grid=(n_rows // rows,),
        in_specs=[
            x_spec,
            _const_spec((bsz, D_RNN)), _const_spec((halo, D_RNN)),
            _const_spec((1, D_MODEL)), _const_spec((1, D_MODEL)),
            _const_spec((D_MODEL, 2 * D_RNN)),
            _const_spec((CONV_W, D_RNN)), _const_spec((1, D_RNN)),
            _const_spec((D_RNN, D_RNN)), _const_spec((1, D_RNN)),
            _const_spec((D_RNN, D_RNN)), _const_spec((1, D_RNN)),
            _const_spec((1, D_RNN)),
            _const_spec((D_RNN, D_MODEL)),
        ],
        out_specs=[
            _row_spec(rows, D_MODEL),
            pl.BlockSpec((bsz, D_RNN), lambda i: (0, 0)),
            pl.BlockSpec((halo, D_RNN), lambda i: (0, 0)),
        ],
        out_shape=[
            jax.ShapeDtypeStruct((n_rows, D_MODEL), F32),
            jax.ShapeDtypeStruct((bsz, D_RNN), F32),
            jax.ShapeDtypeStruct((halo, D_RNN), F32),
        ],
        scratch_shapes=[
            pltpu.VMEM((halo + rows, D_RNN), F32),
            pltpu.VMEM((bsz, D_RNN), F32),
            pltpu.VMEM((rows, D_RNN), F32),
            pltpu.VMEM((rows, D_RNN), F32),
        ],
        compiler_params=_params("arbitrary"),
        name="lru_layer",
    )(x, h0, conv0, g0, g1, w_in, w_conv, b_conv, wa, ba, wx, bx, lam, w_out)


FFN_CHUNKS = ((0, 6 * MXU_DIM), (6 * MXU_DIM, D_FF))


def _ffn_kernel(x_ref, g2_ref, g3_ref, w_in_ref, w_out_ref, out_ref, act_ref, *, out_bsz):
    x = x_ref[...]
    xn = _rms(x, g2_ref[...]).astype(BF16)
    for c0, c1 in FFN_CHUNKS:
        gate = jnp.dot(xn, w_in_ref[:, c0:c1], preferred_element_type=F32)
        up = jnp.dot(xn, w_in_ref[:, D_FF + c0:D_FF + c1], preferred_element_type=F32)
        act_ref[:, c0:c1] = (jax.nn.silu(gate) * up).astype(BF16)
    m = jnp.dot(act_ref[...], w_out_ref[...], preferred_element_type=F32)
    y = x + _rms(m, g3_ref[...])
    if out_bsz:
        out_ref[...] = jnp.swapaxes(y.reshape(y.shape[0] // out_bsz, out_bsz, D_MODEL), 0, 1)
    else:
        out_ref[...] = y


def _ffn_layer(x, g2, g3, w_in, w_out, *, layer, rows, out_bsz=0):
    n_rows = x.shape[0]
    assert n_rows % rows == 0
    if out_bsz:
        steps = rows // out_bsz
        out_spec = pl.BlockSpec((out_bsz, steps, D_MODEL), lambda i: (0, i, 0))
        out_shape = jax.ShapeDtypeStruct((out_bsz, n_rows // out_bsz, D_MODEL), F32)
    else:
        out_spec = _row_spec(rows, D_MODEL)
        out_shape = jax.ShapeDtypeStruct((n_rows, D_MODEL), F32)
    return pl.pallas_call(
        functools.partial(_ffn_kernel, out_bsz=out_bsz),
        grid=(n_rows // rows,),
        in_specs=[
            _row_spec(rows, D_MODEL),
            _const_spec((1, D_MODEL)), _const_spec((1, D_MODEL)),
            pl.BlockSpec((None, D_MODEL, 2 * D_FF), lambda i: (layer, 0, 0), pipeline_mode=pl.Buffered(1)),
            pl.BlockSpec((None, D_FF, D_MODEL), lambda i: (layer, 0, 0), pipeline_mode=pl.Buffered(1)),
        ],
        out_specs=out_spec,
        out_shape=out_shape,
        scratch_shapes=[pltpu.VMEM((rows, D_FF), BF16)],
        compiler_params=_params("parallel"),
        name="ffn_layer",
    )(x, g2, g3, w_in, w_out)


def _s5_disc_kernel(lre_ref, lim_ref, logdt_ref, bre_ref, bim_ref,
                    are_ref, aim_ref, bbre_ref, bbim_ref):
    lre = lre_ref[...]
    lim = lim_ref[...]
    dt = jnp.exp(logdt_ref[...])
    mag = jnp.exp(dt * lre)
    are = mag * jnp.cos(dt * lim)
    aim = mag * jnp.sin(dt * lim)
    are_ref[...] = are
    aim_ref[...] = aim
    nre = are - 1.0
    den = lre * lre + lim * lim
    cre = (nre * lre + aim * lim) / den
    cim = (aim * lre - nre * lim) / den
    shape3 = (S5_GROUPS, S5_GROUP, S5_STATE)
    cre3 = jnp.broadcast_to(cre[:, None, :], shape3).reshape(S5_GROUPS * S5_GROUP, S5_STATE)
    cim3 = jnp.broadcast_to(cim[:, None, :], shape3).reshape(S5_GROUPS * S5_GROUP, S5_STATE)
    bre = bre_ref[...]
    bim = bim_ref[...]
    bbre_ref[...] = cre3 * bre - cim3 * bim
    bbim_ref[...] = cre3 * bim + cim3 * bre


def _s5_discretize(lam_re, lam_im, log_dt, bt_re, bt_im):
    gp = jax.ShapeDtypeStruct((S5_GROUPS, S5_STATE), F32)
    gcp = jax.ShapeDtypeStruct((S5_GROUPS * S5_GROUP, S5_STATE), F32)
    return pl.pallas_call(
        _s5_disc_kernel,
        out_shape=[gp, gp, gcp, gcp],
        name="s5_discretize",
    )(lam_re, lam_im, log_dt, bt_re, bt_im)


def _s5_kernel(x_ref, hre0_ref, him0_ref, g0_ref, g1_ref, are_ref, aim_ref, wb_ref, wc_ref,
               dskip_ref, w_out_ref,
               out_ref, hre_out_ref, him_out_ref,
               hre_ref, him_ref, bu_ref, y_ref, *, bsz, steps):
    rows = bsz * steps
    ts = S5_TILE_STATES

    @pl.when(pl.program_id(0) == 0)
    def _():
        hre_ref[...] = hre0_ref[...]
        him_ref[...] = him0_ref[...]

    x = x_ref[...]
    u = _rms(x, g0_ref[...])
    ub = u.astype(BF16)

    for k in range(S5_KTILES):
        bu_ref[...] = jnp.dot(ub[:, k * MXU_DIM:(k + 1) * MXU_DIM], wb_ref[k],
                              preferred_element_type=F32)
        for c0 in range(0, ts, S5_SCAN_W):
            lanes = slice(k * ts + c0, k * ts + c0 + S5_SCAN_W)
            re_cols = slice(c0, c0 + S5_SCAN_W)
            im_cols = slice(ts + c0, ts + c0 + S5_SCAN_W)
            ar = jnp.broadcast_to(are_ref[:, lanes], (bsz, S5_SCAN_W))
            ai = jnp.broadcast_to(aim_ref[:, lanes], (bsz, S5_SCAN_W))

            def step(t, carry, re_cols=re_cols, im_cols=im_cols, ar=ar, ai=ai):
                hr, hi = carry
                rs = pl.ds(pl.multiple_of(t * bsz, bsz), bsz)
                nr = (ar * hr - ai * hi) + bu_ref[rs, re_cols]
                ni = (ar * hi + ai * hr) + bu_ref[rs, im_cols]
                bu_ref[rs, re_cols] = nr
                bu_ref[rs, im_cols] = ni
                return nr, ni

            hr, hi = lax.fori_loop(0, steps, step, (hre_ref[:, lanes], him_ref[:, lanes]))
            hre_ref[:, lanes] = hr
            him_ref[:, lanes] = hi
        y_ref[:, k * MXU_DIM:(k + 1) * MXU_DIM] = jnp.dot(
            bu_ref[...].astype(BF16), wc_ref[k], preferred_element_type=F32)

    hre_out_ref[...] = hre_ref[...]
    him_out_ref[...] = him_ref[...]

    y = y_ref[...] + dskip_ref[...] * u
    z = jax.nn.gelu(y).astype(BF16)
    zz = jnp.dot(z, w_out_ref[...], preferred_element_type=F32)
    m = zz[:, :D_MODEL] * jax.nn.sigmoid(zz[:, D_MODEL:])
    out_ref[...] = x + _rms(m, g1_ref[...])


def _s5_layer(x, hre0, him0, g0, g1, are, aim, wb, wc, dskip, w_out, *, bsz, steps):
    n_rows = x.shape[0]
    rows = bsz * steps
    n_state = S5_GROUPS * S5_STATE
    assert n_rows % rows == 0
    kern = functools.partial(_s5_kernel, bsz=bsz, steps=steps)
    state = jax.ShapeDtypeStruct((bsz, n_state), F32)
    return pl.pallas_call(
        kern,
        grid=(n_rows // rows,),
        in_specs=[
            _row_spec(rows, D_MODEL),
            _const_spec((bsz, n_state)), _const_spec((bsz, n_state)),
            _const_spec((1, D_MODEL)), _const_spec((1, D_MODEL)),
            _const_spec((1, n_state)), _const_spec((1, n_state)),
            _const_spec((S5_KTILES, MXU_DIM, 2 * S5_TILE_STATES)),
            _const_spec((S5_KTILES, 2 * S5_TILE_STATES, MXU_DIM)),
            _const_spec((1, D_MODEL)),
            _const_spec((D_MODEL, 2 * D_MODEL)),
        ],
        out_specs=[
            _row_spec(rows, D_MODEL),
            pl.BlockSpec((bsz, n_state), lambda i: (0, 0)),
            pl.BlockSpec((bsz, n_state), lambda i: (0, 0)),
        ],
        out_shape=[jax.ShapeDtypeStruct((n_rows, D_MODEL), F32), state, state],
        scratch_shapes=[
            pltpu.VMEM((bsz, n_state), F32),
            pltpu.VMEM((bsz, n_state), F32),
            pltpu.VMEM((rows, 2 * S5_TILE_STATES), F32),
            pltpu.VMEM((rows, D_MODEL), F32),
        ],
        compiler_params=_params("arbitrary"),
        name="s5_layer",
    )(x, hre0, him0, g0, g1, are, aim, wb, wc, dskip, w_out)


def _block_diag(blocks):
    n, r, c = blocks.shape
    eye = jnp.eye(n, dtype=blocks.dtype)
    return (blocks[:, :, None, :] * eye[:, None, :, None]).reshape(n * r, n * c)


def _tile_block_diag(blocks):
    per = S5_GROUPS // S5_KTILES
    g, r, c = blocks.shape
    return jax.vmap(_block_diag)(blocks.reshape(S5_KTILES, per, r, c))


def _row(v):
    return v.reshape(1, -1)


def kernel(x_prompt, x_sample, state_lru_h, state_lru_conv, state_s5_re, state_s5_im, meta_tokens, norm_gains, lru_w_in, lru_w_conv, lru_b_conv, lru_w_a, lru_b_a, lru_w_x, lru_b_x, lru_lambda, lru_w_out, s5_lambda_re, s5_lambda_im, s5_log_dt, s5_b_re, s5_b_im, s5_c_re, s5_c_im, s5_d, s5_w_out, ffn_w_in, ffn_w_out):
    bp, seq, _ = x_prompt.shape
    bs = x_sample.shape[0]
    assert seq % PROMPT_STEPS == 0
    n_state = S5_GROUPS * S5_STATE

    lru_in = lru_w_in[0].astype(BF16)
    wa = _block_diag(lru_w_a[0]).astype(BF16)
    wx = _block_diag(lru_w_x[0]).astype(BF16)
    lru_out = lru_w_out[0].astype(BF16)
    ffn_in = ffn_w_in.astype(BF16)
    ffn_out = ffn_w_out.astype(BF16)
    s5_out = s5_w_out[0].astype(BF16)

    bt_re = s5_b_re[0].transpose(0, 2, 1).reshape(S5_GROUPS * S5_GROUP, S5_STATE)
    bt_im = s5_b_im[0].transpose(0, 2, 1).reshape(S5_GROUPS * S5_GROUP, S5_STATE)
    are, aim, bb_re, bb_im = _s5_discretize(
        s5_lambda_re[0], s5_lambda_im[0], s5_log_dt[0].reshape(S5_GROUPS, 1), bt_re, bt_im)
    shape_b = (S5_GROUPS, S5_GROUP, S5_STATE)
    wb = jnp.concatenate([_tile_block_diag(bb_re.reshape(shape_b)),
                          _tile_block_diag(bb_im.reshape(shape_b))], axis=2).astype(BF16)
    ct_re = s5_c_re[0].transpose(0, 2, 1)
    ct_im = s5_c_im[0].transpose(0, 2, 1)
    wc = jnp.concatenate([_tile_block_diag(ct_re), _tile_block_diag(-ct_im)], axis=1).astype(BF16)
    are = are.reshape(1, n_state)
    aim = aim.reshape(1, n_state)

    g = norm_gains

    def run(x_in, h0, conv0, hre0, him0, bsz, steps, final_ffn=True, out_bsz=0):
        rows = bsz * steps
        x1, h_last, conv_new = _lru_layer(
            x_in, h0, conv0, _row(g[0, 0]), _row(g[0, 1]), lru_in, lru_w_conv[0], _row(lru_b_conv[0]),
            wa, _row(lru_b_a[0]), wx, _row(lru_b_x[0]), _row(lru_lambda[0]), lru_out,
            bsz=bsz, steps=steps)
        x2 = _ffn_layer(x1, _row(g[0, 2]), _row(g[0, 3]), ffn_in, ffn_out, layer=0, rows=rows)
        x3, hre, him = _s5_layer(
            x2, hre0, him0, _row(g[1, 0]), _row(g[1, 1]), are, aim, wb, wc, _row(s5_d[0]), s5_out,
            bsz=bsz, steps=steps)
        x4 = None
        if final_ffn:
            x4 = _ffn_layer(x3, _row(g[1, 2]), _row(g[1, 3]), ffn_in, ffn_out, layer=1, rows=rows,
                            out_bsz=out_bsz)
        return x4, h_last, conv_new, hre, him

    meta_rows = jnp.repeat(meta_tokens.astype(x_prompt.dtype), bp, axis=0)
    zeros_h = jnp.zeros((bp, D_RNN), F32)
    zeros_c = jnp.zeros(((CONV_W - 1) * bp, D_RNN), F32)
    zeros_s = jnp.zeros((bp, n_state), F32)
    _, hm, cm, rem, imm = run(meta_rows, zeros_h, zeros_c, zeros_s, zeros_s, bp, N_META, final_ffn=False)
    y_prompt, hp, cp, rep, imp = run(x_prompt, hm, cm, rem, imm, bp, PROMPT_STEPS, out_bsz=bp)

    xs = x_sample.reshape(bs, D_MODEL)
    conv0_s = state_lru_conv[0].transpose(1, 0, 2).reshape((CONV_W - 1) * bs, D_RNN)
    ys, hs, cs, res, ims = run(xs, state_lru_h[0], conv0_s,
                               state_s5_re[0].reshape(bs, n_state), state_s5_im[0].reshape(bs, n_state),
                               bs, 1)
    y_sample = ys.reshape(bs, 1, D_MODEL)

    def conv_out(c, b):
        return c.reshape(CONV_W - 1, b, D_RNN).transpose(1, 0, 2)[None]

    def s5_out_state(s, b):
        return s.reshape(1, b, S5_GROUPS, S5_STATE)

    return (y_prompt, y_sample,
            hp[None], conv_out(cp, bp), s5_out_state(rep, bp), s5_out_state(imp, bp),
            hs[None], conv_out(cs, bs), s5_out_state(res, bs), s5_out_state(ims, bs))
```

```python
import functools
import math

import jax
import jax.numpy as jnp
from jax import lax
from jax.experimental import pallas as pl
from jax.experimental.pallas import tpu as pltpu

D_MODEL = 1024
N_META = 16
RMS_EPS = 1e-6
D_RNN = 1408
LRU_BLOCKS = 16
LRU_BW = D_RNN // LRU_BLOCKS
CONV_W = 4
LRU_C = 8.0
S5_GROUP = 16
S5_GROUPS = D_MODEL // S5_GROUP
S5_STATE = 64
D_FF = 2816

LANES = 128
MXU_DIM = 256
VMEM_LIMIT_BYTES = 56 * 1024 * 1024

S5_KTILES = D_MODEL // MXU_DIM
S5_TILE_STATES = (MXU_DIM // S5_GROUP) * S5_STATE
S5_SCAN_W = 512
PROMPT_STEPS = 64

BF16 = jnp.bfloat16
F32 = jnp.float32


def _rms(x, g):
    return x * lax.rsqrt(jnp.mean(x * x, axis=-1, keepdims=True) + RMS_EPS) * g


def _const_spec(shape):
    nd = len(shape)
    return pl.BlockSpec(shape, lambda i: (0,) * nd, pipeline_mode=pl.Buffered(1))


def _row_spec(rows, cols):
    return pl.BlockSpec((rows, cols), lambda i: (i, 0))


def _params(sem):
    return pltpu.CompilerParams(dimension_semantics=(sem,), vmem_limit_bytes=VMEM_LIMIT_BYTES)


def _gate_bands():
    bands = []
    for n0 in range(0, D_RNN, MXU_DIM):
        n1 = min(n0 + MXU_DIM, D_RNN)
        k0 = (n0 // LRU_BW) * LRU_BW
        k1 = -(-n1 // LRU_BW) * LRU_BW
        k0 = (k0 // LANES) * LANES
        k1 = min(D_RNN, -(-k1 // LANES) * LANES)
        bands.append((n0, n1, k0, k1))
    return tuple(bands)


GATE_BANDS = _gate_bands()


def _lru_kernel(x_ref, h0_ref, conv0_ref, g0_ref, g1_ref, w_in_ref, w_conv_ref, b_conv_ref,
                wa_ref, ba_ref, wx_ref, bx_ref, lam_ref, w_out_ref,
                out_ref, hlast_ref, convnew_ref,
                ext_ref, h_ref, gb_ref, xc_ref, xcb_ref, a_ref, b_ref, *, bsz, steps, batch_major_in):
    rows = bsz * steps
    halo = (CONV_W - 1) * bsz

    @pl.when(pl.program_id(0) == 0)
    def _():
        ext_ref[0:halo, :] = conv0_ref[...]
        h_ref[...] = h0_ref[...]

    if batch_major_in:
        x = jnp.swapaxes(x_ref[...], 0, 1).reshape(rows, D_MODEL)
    else:
        x = x_ref[...]
    xn = _rms(x, g0_ref[...]).astype(BF16)

    def project(j):
        n0, n1, _, _ = GATE_BANDS[j]
        cols = slice(n0, n1)
        ext_ref[halo:halo + rows, cols] = jnp.dot(xn, w_in_ref[:, n0:n1], preferred_element_type=F32)
        gb_ref[:, cols] = jnp.dot(xn, w_in_ref[:, D_RNN + n0:D_RNN + n1], preferred_element_type=F32)
        xc = b_conv_ref[:, cols] + w_conv_ref[0:1, cols] * ext_ref[0:rows, cols]
        for k in range(1, CONV_W):
            xc = xc + w_conv_ref[k:k + 1, cols] * ext_ref[k * bsz:k * bsz + rows, cols]
        xc_ref[:, cols] = xc
        xcb_ref[:, cols] = xc.astype(BF16)

    def recur(j):
        n0, n1, k0, k1 = GATE_BANDS[j]
        cols = slice(n0, n1)
        xcb = xcb_ref[:, k0:k1]
        r = jax.nn.sigmoid(jnp.dot(xcb, wa_ref[k0:k1, cols], preferred_element_type=F32) + ba_ref[:, cols])
        ig = jax.nn.sigmoid(jnp.dot(xcb, wx_ref[k0:k1, cols], preferred_element_type=F32) + bx_ref[:, cols])
        neg_lam = -lam_ref[:, cols]
        softplus = jnp.maximum(neg_lam, 0.0) + jnp.log1p(jnp.exp(-jnp.abs(neg_lam)))
        log_a = (-LRU_C) * r * softplus
        a = jnp.exp(log_a)
        a_ref[:, cols] = a
        b_ref[:, cols] = jnp.sqrt(-jnp.tanh(log_a) * (a * a + 1.0)) * (ig * xc_ref[:, cols])
        h = h_ref[:, cols]
        for t in range(steps):
            rs = slice(t * bsz, (t + 1) * bsz)
            h = a_ref[rs, cols] * h + b_ref[rs, cols]
            b_ref[rs, cols] = h
        h_ref[:, cols] = h
        hlast_ref[:, cols] = h

    def out_part(j):
        n0, n1, _, _ = GATE_BANDS[j]
        cols = slice(n0, n1)
        y = (b_ref[:, cols] * jax.nn.gelu(gb_ref[:, cols])).astype(BF16)
        return jnp.dot(y, w_out_ref[n0:n1, :], preferred_element_type=F32)

    n_tiles = len(GATE_BANDS)
    m = None
    project(0)
    for j in range(n_tiles):
        if j + 1 < n_tiles:
            project(j + 1)
        recur(j)
        if j >= 1:
            part = out_part(j - 1)
            m = part if m is None else m + part
    m = m + out_part(n_tiles - 1)
    new_halo = ext_ref[rows:rows + halo, :]
    ext_ref[0:halo, :] = new_halo
    convnew_ref[...] = new_halo
    out_ref[...] = x + _rms(m, g1_ref[...])


def _lru_layer(x, h0, conv0, g0, g1, w_in, w_conv, b_conv, wa, ba, wx, bx, lam, w_out, *, bsz, steps):
    batch_major_in = x.ndim == 3
    rows = bsz * steps
    halo = (CONV_W - 1) * bsz
    if batch_major_in:
        assert x.shape[0] == bsz and x.shape[1] % steps == 0
        n_rows = bsz * x.shape[1]
        x_spec = pl.BlockSpec((bsz, steps, D_MODEL), lambda i: (0, i, 0))
    else:
        n_rows = x.shape[0]
        assert n_rows % rows == 0
        x_spec = _row_spec(rows, D_MODEL)
    kern = functools.partial(_lru_kernel, bsz=bsz, steps=steps, batch_major_in=batch_major_in)
    return pl.pallas_call(
        kern,
        grid=(n_rows // rows,),
        in_specs=[
            x_spec,
            _const_spec((bsz, D_RNN)), _const_spec((halo, D_RNN)),
            _const_spec((1, D_MODEL)), _const_spec((1, D_MODEL)),
            _const_spec((D_MODEL, 2 * D_RNN)),
            _const_spec((CONV_W, D_RNN)), _const_spec((1, D_RNN)),
            _const_spec((D_RNN, D_RNN)), _const_spec((1, D_RNN)),
            _const_spec((D_RNN, D_RNN)), _const_spec((1, D_RNN)),
            _const_spec((1, D_RNN)),
            _const_spec((D_RNN, D_MODEL)),
        ],
        out_specs=[
            _row_spec(rows, D_MODEL),
            pl.BlockSpec((bsz, D_RNN), lambda i: (0, 0)),
            pl.BlockSpec((halo, D_RNN), lambda i: (0, 0)),
        ],
        out_shape=[
            jax.ShapeDtypeStruct((n_rows, D_MODEL), F32),
            jax.ShapeDtypeStruct((bsz, D_RNN), F32),
            jax.ShapeDtypeStruct((halo, D_RNN), F32),
        ],
        scratch_shapes=[
            pltpu.VMEM((halo + rows, D_RNN), F32),
            pltpu.VMEM((bsz, D_RNN), F32),
            pltpu.VMEM((rows, D_RNN), F32),
            pltpu.VMEM((rows, D_RNN), F32),
            pltpu.VMEM((rows, D_RNN), BF16),
            pltpu.VMEM((rows, D_RNN), F32),
            pltpu.VMEM((rows, D_RNN), F32),
        ],
        compiler_params=_params("arbitrary"),
        name="lru_layer",
    )(x, h0, conv0, g0, g1, w_in, w_conv, b_conv, wa, ba, wx, bx, lam, w_out)


FFN_CHUNKS = ((0, 6 * MXU_DIM), (6 * MXU_DIM, D_FF))


def _ffn_kernel(x_ref, g2_ref, g3_ref, w_in_ref, w_out_ref, out_ref, act_ref, *, out_bsz):
    x = x_ref[...]
    xn = _rms(x, g2_ref[...]).astype(BF16)
    for c0, c1 in FFN_CHUNKS:
        gate = jnp.dot(xn, w_in_ref[:, c0:c1], preferred_element_type=F32)
        up = jnp.dot(xn, w_in_ref[:, D_FF + c0:D_FF + c1], preferred_element_type=F32)
        act_ref[:, c0:c1] = (jax.nn.silu(gate) * up).astype(BF16)
    m = jnp.dot(act_ref[...], w_out_ref[...], preferred_element_type=F32)
    y = x + _rms(m, g3_ref[...])
    if out_bsz:
        out_ref[...] = jnp.swapaxes(y.reshape(y.shape[0] // out_bsz, out_bsz, D_MODEL), 0, 1)
    else:
        out_ref[...] = y


def _ffn_layer(x, g2, g3, w_in, w_out, *, layer, rows, out_bsz=0):
    n_rows = x.shape[0]
    assert n_rows % rows == 0
    if out_bsz:
        steps = rows // out_bsz
        out_spec = pl.BlockSpec((out_bsz, steps, D_MODEL), lambda i: (0, i, 0))
        out_shape = jax.ShapeDtypeStruct((out_bsz, n_rows // out_bsz, D_MODEL), F32)
    else:
        out_spec = _row_spec(rows, D_MODEL)
        out_shape = jax.ShapeDtypeStruct((n_rows, D_MODEL), F32)
    return pl.pallas_call(
        functools.partial(_ffn_kernel, out_bsz=out_bsz),
        grid=(n_rows // rows,),
        in_specs=[
            _row_spec(rows, D_MODEL),
            _const_spec((1, D_MODEL)), _const_spec((1, D_MODEL)),
            pl.BlockSpec((None, D_MODEL, 2 * D_FF), lambda i: (layer, 0, 0), pipeline_mode=pl.Buffered(1)),
            pl.BlockSpec((None, D_FF, D_MODEL), lambda i: (layer, 0, 0), pipeline_mode=pl.Buffered(1)),
        ],
        out_specs=out_spec,
        out_shape=out_shape,
        scratch_shapes=[pltpu.VMEM((rows, D_FF), BF16)],
        compiler_params=_params("parallel"),
        name="ffn_layer",
    )(x, g2, g3, w_in, w_out)


def _s5_disc_kernel(lre_ref, lim_ref, logdt_ref, bre_ref, bim_ref,
                    are_ref, aim_ref, bbre_ref, bbim_ref):
    lre = lre_ref[...]
    lim = lim_ref[...]
    dt = jnp.exp(logdt_ref[...])
    mag = jnp.exp(dt * lre)
    are = mag * jnp.cos(dt * lim)
    aim = mag * jnp.sin(dt * lim)
    are_ref[...] = are
    aim_ref[...] = aim
    nre = are - 1.0
    den = lre * lre + lim * lim
    cre = (nre * lre + aim * lim) / den
    cim = (aim * lre - nre * lim) / den
    shape3 = (S5_GROUPS, S5_GROUP, S5_STATE)
    cre3 = jnp.broadcast_to(cre[:, None, :], shape3).reshape(S5_GROUPS * S5_GROUP, S5_STATE)
    cim3 = jnp.broadcast_to(cim[:, None, :], shape3).reshape(S5_GROUPS * S5_GROUP, S5_STATE)
    bre = bre_ref[...]
    bim = bim_ref[...]
    bbre_ref[...] = cre3 * bre - cim3 * bim
    bbim_ref[...] = cre3 * bim + cim3 * bre


def _s5_discretize(lam_re, lam_im, log_dt, bt_re, bt_im):
    gp = jax.ShapeDtypeStruct((S5_GROUPS, S5_STATE), F32)
    gcp = jax.ShapeDtypeStruct((S5_GROUPS * S5_GROUP, S5_STATE), F32)
    return pl.pallas_call(
        _s5_disc_kernel,
        out_shape=[gp, gp, gcp, gcp],
        name="s5_discretize",
    )(lam_re, lam_im, log_dt, bt_re, bt_im)


def _s5_kernel(x_ref, hre0_ref, him0_ref, g0_ref, g1_ref, are_ref, aim_ref, wb_ref, wc_ref,
               dskip_ref, w_out_ref,
               out_ref, hre_out_ref, him_out_ref,
               hre_ref, him_ref, bu_ref, y_ref, *, bsz, steps):
    rows = bsz * steps
    ts = S5_TILE_STATES

    @pl.when(pl.program_id(0) == 0)
    def _():
        hre_ref[...] = hre0_ref[...]
        him_ref[...] = him0_ref[...]

    x = x_ref[...]
    u = _rms(x, g0_ref[...])
    ub = u.astype(BF16)

    for k in range(S5_KTILES):
        bu = bu_ref.at[k % 2]
        bu[...] = jnp.dot(ub[:, k * MXU_DIM:(k + 1) * MXU_DIM], wb_ref[k], preferred_element_type=F32)
        for c0 in range(0, ts, S5_SCAN_W):
            lanes = slice(k * ts + c0, k * ts + c0 + S5_SCAN_W)
            re_cols = slice(c0, c0 + S5_SCAN_W)
            im_cols = slice(ts + c0, ts + c0 + S5_SCAN_W)
            ar = jnp.broadcast_to(are_ref[:, lanes], (bsz, S5_SCAN_W))
            ai = jnp.broadcast_to(aim_ref[:, lanes], (bsz, S5_SCAN_W))
            hr = hre_ref[:, lanes]
            hi = him_ref[:, lanes]
            for t in range(steps):
                rs = slice(t * bsz, (t + 1) * bsz)
                hr, hi = ((ar * hr - ai * hi) + bu[rs, re_cols],
                          (ar * hi + ai * hr) + bu[rs, im_cols])
                bu[rs, re_cols] = hr
                bu[rs, im_cols] = hi
            hre_ref[:, lanes] = hr
            him_ref[:, lanes] = hi
        y_ref[:, k * MXU_DIM:(k + 1) * MXU_DIM] = jnp.dot(
            bu[...].astype(BF16), wc_ref[k], preferred_element_type=F32)

    hre_out_ref[...] = hre_ref[...]
    him_out_ref[...] = him_ref[...]

    y = y_ref[...] + dskip_ref[...] * u
    z = jax.nn.gelu(y).astype(BF16)
    zz = jnp.dot(z, w_out_ref[...], preferred_element_type=F32)
    m = zz[:, :D_MODEL] * jax.nn.sigmoid(zz[:, D_MODEL:])
    out_ref[...] = x + _rms(m, g1_ref[...])


def _s5_layer(x, hre0, him0, g0, g1, are, aim, wb, wc, dskip, w_out, *, bsz, steps):
    n_rows = x.shape[0]
    rows = bsz * steps
    n_state = S5_GROUPS * S5_STATE
    assert n_rows % rows == 0
    kern = functools.partial(_s5_kernel, bsz=bsz, steps=steps)
    state = jax.ShapeDtypeStruct((bsz, n_state), F32)
    return pl.pallas_call(
        kern,
        grid=(n_rows // rows,),
        in_specs=[
            _row_spec(rows, D_MODEL),
            _const_spec((bsz, n_state)), _const_spec((bsz, n_state)),
            _const_spec((1, D_MODEL)), _const_spec((1, D_MODEL)),
            _const_spec((1, n_state)), _const_spec((1, n_state)),
            _const_spec((S5_KTILES, MXU_DIM, 2 * S5_TILE_STATES)),
            _const_spec((S5_KTILES, 2 * S5_TILE_STATES, MXU_DIM)),
            _const_spec((1, D_MODEL)),
            _const_spec((D_MODEL, 2 * D_MODEL)),
        ],
        out_specs=[
            _row_spec(rows, D_MODEL),
            pl.BlockSpec((bsz, n_state), lambda i: (0, 0)),
            pl.BlockSpec((bsz, n_state), lambda i: (0, 0)),
        ],
        out_shape=[jax.ShapeDtypeStruct((n_rows, D_MODEL), F32), state, state],
        scratch_shapes=[
            pltpu.VMEM((bsz, n_state), F32),
            pltpu.VMEM((bsz, n_state), F32),
            pltpu.VMEM((2, rows, 2 * S5_TILE_STATES), F32),
            pltpu.VMEM((rows, D_MODEL), F32),
        ],
        compiler_params=_params("arbitrary"),
        name="s5_layer",
    )(x, hre0, him0, g0, g1, are, aim, wb, wc, dskip, w_out)


def _block_diag(blocks):
    n, r, c = blocks.shape
    eye = jnp.eye(n, dtype=blocks.dtype)
    return (blocks[:, :, None, :] * eye[:, None, :, None]).reshape(n * r, n * c)


def _tile_block_diag(blocks):
    per = S5_GROUPS // S5_KTILES
    g, r, c = blocks.shape
    return jax.vmap(_block_diag)(blocks.reshape(S5_KTILES, per, r, c))


def _row(v):
    return v.reshape(1, -1)


def kernel(x_prompt, x_sample, state_lru_h, state_lru_conv, state_s5_re, state_s5_im, meta_tokens, norm_gains, lru_w_in, lru_w_conv, lru_b_conv, lru_w_a, lru_b_a, lru_w_x, lru_b_x, lru_lambda, lru_w_out, s5_lambda_re, s5_lambda_im, s5_log_dt, s5_b_re, s5_b_im, s5_c_re, s5_c_im, s5_d, s5_w_out, ffn_w_in, ffn_w_out):
    bp, seq, _ = x_prompt.shape
    bs = x_sample.shape[0]
    assert seq % PROMPT_STEPS == 0
    n_state = S5_GROUPS * S5_STATE

    lru_in = lru_w_in[0].astype(BF16)
    wa = _block_diag(lru_w_a[0]).astype(BF16)
    wx = _block_diag(lru_w_x[0]).astype(BF16)
    lru_out = lru_w_out[0].astype(BF16)
    ffn_in = ffn_w_in.astype(BF16)
    ffn_out = ffn_w_out.astype(BF16)
    s5_out = s5_w_out[0].astype(BF16)

    bt_re = s5_b_re[0].transpose(0, 2, 1).reshape(S5_GROUPS * S5_GROUP, S5_STATE)
    bt_im = s5_b_im[0].transpose(0, 2, 1).reshape(S5_GROUPS * S5_GROUP, S5_STATE)
    are, aim, bb_re, bb_im = _s5_discretize(
        s5_lambda_re[0], s5_lambda_im[0], s5_log_dt[0].reshape(S5_GROUPS, 1), bt_re, bt_im)
    shape_b = (S5_GROUPS, S5_GROUP, S5_STATE)
    wb = jnp.concatenate([_tile_block_diag(bb_re.reshape(shape_b)),
                          _tile_block_diag(bb_im.reshape(shape_b))], axis=2).astype(BF16)
    ct_re = s5_c_re[0].transpose(0, 2, 1)
    ct_im = s5_c_im[0].transpose(0, 2, 1)
    wc = jnp.concatenate([_tile_block_diag(ct_re), _tile_block_diag(-ct_im)], axis=1).astype(BF16)
    are = are.reshape(1, n_state)
    aim = aim.reshape(1, n_state)

    g = norm_gains

    def run(x_in, h0, conv0, hre0, him0, bsz, steps, final_ffn=True, out_bsz=0):
        rows = bsz * steps
        x1, h_last, conv_new = _lru_layer(
            x_in, h0, conv0, _row(g[0, 0]), _row(g[0, 1]), lru_in, lru_w_conv[0], _row(lru_b_conv[0]),
            wa, _row(lru_b_a[0]), wx, _row(lru_b_x[0]), _row(lru_lambda[0]), lru_out,
            bsz=bsz, steps=steps)
        x2 = _ffn_layer(x1, _row(g[0, 2]), _row(g[0, 3]), ffn_in, ffn_out, layer=0, rows=rows)
        x3, hre, him = _s5_layer(
            x2, hre0, him0, _row(g[1, 0]), _row(g[1, 1]), are, aim, wb, wc, _row(s5_d[0]), s5_out,
            bsz=bsz, steps=steps)
        x4 = None
        if final_ffn:
            x4 = _ffn_layer(x3, _row(g[1, 2]), _row(g[1, 3]), ffn_in, ffn_out, layer=1, rows=rows,
                            out_bsz=out_bsz)
        return x4, h_last, conv_new, hre, him

    meta_rows = jnp.repeat(meta_tokens.astype(x_prompt.dtype), bp, axis=0)
    zeros_h = jnp.zeros((bp, D_RNN), F32)
    zeros_c = jnp.zeros(((CONV_W - 1) * bp, D_RNN), F32)
    zeros_s = jnp.zeros((bp, n_state), F32)
    _, hm, cm, rem, imm = run(meta_rows, zeros_h, zeros_c, zeros_s, zeros_s, bp, N_META, final_ffn=False)
    y_prompt, hp, cp, rep, imp = run(x_prompt, hm, cm, rem, imm, bp, PROMPT_STEPS, out_bsz=bp)

    xs = x_sample.reshape(bs, D_MODEL)
    conv0_s = state_lru_conv[0].transpose(1, 0, 2).reshape((CONV_W - 1) * bs, D_RNN)
    ys, hs, cs, res, ims = run(xs, state_lru_h[0], conv0_s,
                               state_s5_re[0].reshape(bs, n_state), state_s5_im[0].reshape(bs, n_state),
                               bs, 1)
    y_sample = ys.reshape(bs, 1, D_MODEL)

    def conv_out(c, b):
        return c.reshape(CONV_W - 1, b, D_RNN).transpose(1, 0, 2)[None]

    def s5_out_state(s, b):
        return s.reshape(1, b, S5_GROUPS, S5_STATE)

    return (y_prompt, y_sample,
            hp[None], conv_out(cp, bp), s5_out_state(rep, bp), s5_out_state(imp, bp),
            hs[None], conv_out(cs, bs), s5_out_state(res, bs), s5_out_state(ims, bs))
```

```python
import functools
import math

import jax
import jax.numpy as jnp
from jax import lax
from jax.experimental import pallas as pl
from jax.experimental.pallas import tpu as pltpu

D_MODEL = 1024
N_META = 16
RMS_EPS = 1e-6
D_RNN = 1408
LRU_BLOCKS = 16
LRU_BW = D_RNN // LRU_BLOCKS
CONV_W = 4
LRU_C = 8.0
S5_GROUP = 16
S5_GROUPS = D_MODEL // S5_GROUP
S5_STATE = 64
D_FF = 2816

LANES = 128
MXU_DIM = 256
VMEM_LIMIT_BYTES = 56 * 1024 * 1024

S5_KTILES = D_MODEL // MXU_DIM
S5_TILE_STATES = (MXU_DIM // S5_GROUP) * S5_STATE
S5_SCAN_W = 512
PROMPT_STEPS = 64

BF16 = jnp.bfloat16
F32 = jnp.float32


def _rms(x, g):
    return x * lax.rsqrt(jnp.mean(x * x, axis=-1, keepdims=True) + RMS_EPS) * g


def _const_spec(shape):
    nd = len(shape)
    return pl.BlockSpec(shape, lambda i: (0,) * nd, pipeline_mode=pl.Buffered(1))


def _row_spec(rows, cols):
    return pl.BlockSpec((rows, cols), lambda i: (i, 0))


def _params(sem):
    return pltpu.CompilerParams(dimension_semantics=(sem,), vmem_limit_bytes=VMEM_LIMIT_BYTES)


def _gate_bands():
    bands = []
    for n0 in range(0, D_RNN, MXU_DIM):
        n1 = min(n0 + MXU_DIM, D_RNN)
        k0 = (n0 // LRU_BW) * LRU_BW
        k1 = -(-n1 // LRU_BW) * LRU_BW
        k0 = (k0 // LANES) * LANES
        k1 = min(D_RNN, -(-k1 // LANES) * LANES)
        bands.append((n0, n1, k0, k1))
    return tuple(bands)


GATE_BANDS = _gate_bands()


def _lru_kernel(x_ref, h0_ref, conv0_ref, g0_ref, g1_ref, w_in_ref, w_conv_ref, b_conv_ref,
                wa_ref, ba_ref, wx_ref, bx_ref, lam_ref, w_out_ref,
                out_ref, hlast_ref, convnew_ref,
                ext_ref, h_ref, gb_ref, xc_ref, xcb_ref, a_ref, b_ref, *, bsz, steps, batch_major_in):
    rows = bsz * steps
    halo = (CONV_W - 1) * bsz

    @pl.when(pl.program_id(0) == 0)
    def _():
        ext_ref[0:halo, :] = conv0_ref[...]
        h_ref[...] = h0_ref[...]

    if batch_major_in:
        x = jnp.swapaxes(x_ref[...], 0, 1).reshape(rows, D_MODEL)
    else:
        x = x_ref[...]
    xn = _rms(x, g0_ref[...]).astype(BF16)

    def project(j):
        n0, n1, _, _ = GATE_BANDS[j]
        cols = slice(n0, n1)
        ext_ref[halo:halo + rows, cols] = jnp.dot(xn, w_in_ref[:, n0:n1], preferred_element_type=F32)
        gb_ref[:, cols] = jnp.dot(xn, w_in_ref[:, D_RNN + n0:D_RNN + n1], preferred_element_type=F32)
        xc = b_conv_ref[:, cols] + w_conv_ref[0:1, cols] * ext_ref[0:rows, cols]
        for k in range(1, CONV_W):
            xc = xc + w_conv_ref[k:k + 1, cols] * ext_ref[k * bsz:k * bsz + rows, cols]
        xc_ref[:, cols] = xc
        xcb_ref[:, cols] = xc.astype(BF16)

    def recur(j):
        n0, n1, k0, k1 = GATE_BANDS[j]
        cols = slice(n0, n1)
        xcb = xcb_ref[:, k0:k1]
        r = jax.nn.sigmoid(jnp.dot(xcb, wa_ref[k0:k1, cols], preferred_element_type=F32) + ba_ref[:, cols])
        ig = jax.nn.sigmoid(jnp.dot(xcb, wx_ref[k0:k1, cols], preferred_element_type=F32) + bx_ref[:, cols])
        neg_lam = -lam_ref[:, cols]
        softplus = jnp.maximum(neg_lam, 0.0) + jnp.log1p(jnp.exp(-jnp.abs(neg_lam)))
        log_a = (-LRU_C) * r * softplus
        a = jnp.exp(log_a)
        a_ref[:, cols] = a
        s = -jnp.tanh(log_a) * (a * a + 1.0)
        root = jnp.where(s > 0.0, s * lax.rsqrt(s), 0.0)
        b_ref[:, cols] = root * (ig * xc_ref[:, cols])
        h = h_ref[:, cols]
        for t in range(steps):
            rs = slice(t * bsz, (t + 1) * bsz)
            h = a_ref[rs, cols] * h + b_ref[rs, cols]
            b_ref[rs, cols] = h
        h_ref[:, cols] = h
        hlast_ref[:, cols] = h

    def out_part(j):
        n0, n1, _, _ = GATE_BANDS[j]
        cols = slice(n0, n1)
        y = (b_ref[:, cols] * jax.nn.gelu(gb_ref[:, cols])).astype(BF16)
        return jnp.dot(y, w_out_ref[n0:n1, :], preferred_element_type=F32)

    n_tiles = len(GATE_BANDS)
    m = None
    project(0)
    for j in range(n_tiles):
        if j + 1 < n_tiles:
            project(j + 1)
        recur(j)
        if j >= 1:
            part = out_part(j - 1)
            m = part if m is None else m + part
    m = m + out_part(n_tiles - 1)
    new_halo = ext_ref[rows:rows + halo, :]
    ext_ref[0:halo, :] = new_halo
    convnew_ref[...] = new_halo
    out_ref[...] = x + _rms(m, g1_ref[...])


def _lru_layer(x, h0, conv0, g0, g1, w_in, w_conv, b_conv, wa, ba, wx, bx, lam, w_out, *, bsz, steps):
    batch_major_in = x.ndim == 3
    rows = bsz * steps
    halo = (CONV_W - 1) * bsz
    if batch_major_in:
        assert x.shape[0] == bsz and x.shape[1] % steps == 0
        n_rows = bsz * x.shape[1]
        x_spec = pl.BlockSpec((bsz, steps, D_MODEL), lambda i: (0, i, 0))
    else:
        n_rows = x.shape[0]
        assert n_rows % rows == 0
        x_spec = _row_spec(rows, D_MODEL)
    kern = functools.partial(_lru_kernel, bsz=bsz, steps=steps, batch_major_in=batch_major_in)
    return pl.pallas_call(
        kern,
        grid=(n_rows // rows,),
        in_specs=[
            x_spec,
            _const_spec((bsz, D_RNN)), _const_spec((halo, D_RNN)),
            _const_spec((1, D_MODEL)), _const_spec((1, D_MODEL)),
            _const_spec((D_MODEL, 2 * D_RNN)),
            _const_spec((CONV_W, D_RNN)), _const_spec((1, D_RNN)),
            _const_spec((D_RNN, D_RNN)), _const_spec((1, D_RNN)),
            _const_spec((D_RNN, D_RNN)), _const_spec((1, D_RNN)),
            _const_spec((1, D_RNN)),
            _const_spec((D_RNN, D_MODEL)),
        ],
        out_specs=[
            _row_spec(rows, D_MODEL),
            pl.BlockSpec((bsz, D_RNN), lambda i: (0, 0)),
            pl.BlockSpec((halo, D_RNN), lambda i: (0, 0)),
        ],
        out_shape=[
            jax.ShapeDtypeStruct((n_rows, D_MODEL), F32),
            jax.ShapeDtypeStruct((bsz, D_RNN), F32),
            jax.ShapeDtypeStruct((halo, D_RNN), F32),
        ],
        scratch_shapes=[
            pltpu.VMEM((halo + rows, D_RNN), F32),
            pltpu.VMEM((bsz, D_RNN), F32),
            pltpu.VMEM((rows, D_RNN), F32),
            pltpu.VMEM((rows, D_RNN), F32),
            pltpu.VMEM((rows, D_RNN), BF16),
            pltpu.VMEM((rows, D_RNN), F32),
            pltpu.VMEM((rows, D_RNN), F32),
        ],
        compiler_params=_params("arbitrary"),
        name="lru_layer",
    )(x, h0, conv0, g0, g1, w_in, w_conv, b_conv, wa, ba, wx, bx, lam, w_out)


FFN_CHUNKS = ((0, 6 * MXU_DIM), (6 * MXU_DIM, D_FF))


def _ffn_kernel(x_ref, g2_ref, g3_ref, w_in_ref, w_out_ref, out_ref, act_ref, *, out_bsz):
    x = x_ref[...]
    xn = _rms(x, g2_ref[...]).astype(BF16)
    for c0, c1 in FFN_CHUNKS:
        gate = jnp.dot(xn, w_in_ref[:, c0:c1], preferred_element_type=F32)
        up = jnp.dot(xn, w_in_ref[:, D_FF + c0:D_FF + c1], preferred_element_type=F32)
        act_ref[:, c0:c1] = (jax.nn.silu(gate) * up).astype(BF16)
    m = jnp.dot(act_ref[...], w_out_ref[...], preferred_element_type=F32)
    y = x + _rms(m, g3_ref[...])
    if out_bsz:
        out_ref[...] = jnp.swapaxes(y.reshape(y.shape[0] // out_bsz, out_bsz, D_MODEL), 0, 1)
    else:
        out_ref[...] = y


def _ffn_layer(x, g2, g3, w_in, w_out, *, layer, rows, out_bsz=0):
    n_rows = x.shape[0]
    assert n_rows % rows == 0
    if out_bsz:
        steps = rows // out_bsz
        out_spec = pl.BlockSpec((out_bsz, steps, D_MODEL), lambda i: (0, i, 0))
        out_shape = jax.ShapeDtypeStruct((out_bsz, n_rows // out_bsz, D_MODEL), F32)
    else:
        out_spec = _row_spec(rows, D_MODEL)
        out_shape = jax.ShapeDtypeStruct((n_rows, D_MODEL), F32)
    return pl.pallas_call(
        functools.partial(_ffn_kernel, out_bsz=out_bsz),
        grid=(n_rows // rows,),
        in_specs=[
            _row_spec(rows, D_MODEL),
            _const_spec((1, D_MODEL)), _const_spec((1, D_MODEL)),
            pl.BlockSpec((None, D_MODEL, 2 * D_FF), lambda i: (layer, 0, 0), pipeline_mode=pl.Buffered(1)),
            pl.BlockSpec((None, D_FF, D_MODEL), lambda i: (layer, 0, 0), pipeline_mode=pl.Buffered(1)),
        ],
        out_specs=out_spec,
        out_shape=out_shape,
        scratch_shapes=[pltpu.VMEM((rows, D_FF), BF16)],
        compiler_params=_params("parallel"),
        name="ffn_layer",
    )(x, g2, g3, w_in, w_out)


def _s5_disc_kernel(lre_ref, lim_ref, logdt_ref, bre_ref, bim_ref, cre_in_ref, cim_in_ref,
                    are_ref, aim_ref, wb_ref, wc_ref):
    lre = lre_ref[...]
    lim = lim_ref[...]
    dt = jnp.exp(logdt_ref[...])
    mag = jnp.exp(dt * lre)
    are = mag * jnp.cos(dt * lim)
    aim = mag * jnp.sin(dt * lim)
    are_ref[...] = are
    aim_ref[...] = aim
    nre = are - 1.0
    den = lre * lre + lim * lim
    cre = (nre * lre + aim * lim) / den
    cim = (aim * lre - nre * lim) / den
    shape3 = (S5_GROUPS, S5_GROUP, S5_STATE)
    cre3 = jnp.broadcast_to(cre[:, None, :], shape3).reshape(S5_GROUPS * S5_GROUP, S5_STATE)
    cim3 = jnp.broadcast_to(cim[:, None, :], shape3).reshape(S5_GROUPS * S5_GROUP, S5_STATE)
    bre = bre_ref[...]
    bim = bim_ref[...]
    bbre = (cre3 * bre - cim3 * bim).astype(BF16)
    bbim = (cre3 * bim + cim3 * bre).astype(BF16)

    ts = S5_TILE_STATES
    per = MXU_DIM // S5_GROUP
    sel = (lax.broadcasted_iota(jnp.int32, (S5_STATE, ts), 0)
           == lax.broadcasted_iota(jnp.int32, (S5_STATE, ts), 1) % S5_STATE).astype(BF16)
    b_mask = (lax.broadcasted_iota(jnp.int32, (MXU_DIM, ts), 0) // S5_GROUP
              == lax.broadcasted_iota(jnp.int32, (MXU_DIM, ts), 1) // S5_STATE)
    zre = jnp.dot(bbre, sel, preferred_element_type=F32)
    zim = jnp.dot(bbim, sel, preferred_element_type=F32)
    for k in range(S5_KTILES):
        rws = slice(k * MXU_DIM, (k + 1) * MXU_DIM)
        wb_ref[k, :, 0:ts] = jnp.where(b_mask, zre[rws], 0.0).astype(BF16)
        wb_ref[k, :, ts:2 * ts] = jnp.where(b_mask, zim[rws], 0.0).astype(BF16)
    c_mask = (lax.broadcasted_iota(jnp.int32, (ts, MXU_DIM), 0) // S5_STATE
              == lax.broadcasted_iota(jnp.int32, (ts, MXU_DIM), 1) // S5_GROUP)
    sel_t = (lax.broadcasted_iota(jnp.int32, (ts, S5_STATE), 0) % S5_STATE
             == lax.broadcasted_iota(jnp.int32, (ts, S5_STATE), 1)).astype(BF16)
    nt = (((1,), (1,)), ((), ()))
    for k in range(S5_KTILES):
        rws = slice(k * per * S5_GROUP, (k + 1) * per * S5_GROUP)
        cre_k = lax.dot_general(sel_t, cre_in_ref[rws, :].astype(BF16), nt, preferred_element_type=F32)
        cim_k = lax.dot_general(sel_t, cim_in_ref[rws, :].astype(BF16), nt, preferred_element_type=F32)
        wc_ref[k, 0:ts, :] = jnp.where(c_mask, cre_k, 0.0).astype(BF16)
        wc_ref[k, ts:2 * ts, :] = jnp.where(c_mask, -cim_k, 0.0).astype(BF16)


def _s5_weights(lam_re, lam_im, log_dt, bt_re, bt_im, c_re, c_im):
    gp = jax.ShapeDtypeStruct((S5_GROUPS, S5_STATE), F32)
    return pl.pallas_call(
        _s5_disc_kernel,
        out_shape=[gp, gp,
                   jax.ShapeDtypeStruct((S5_KTILES, MXU_DIM, 2 * S5_TILE_STATES), BF16),
                   jax.ShapeDtypeStruct((S5_KTILES, 2 * S5_TILE_STATES, MXU_DIM), BF16)],
        compiler_params=pltpu.CompilerParams(vmem_limit_bytes=VMEM_LIMIT_BYTES),
        name="s5_weights",
    )(lam_re, lam_im, log_dt, bt_re, bt_im, c_re, c_im)


def _block_id(idx):
    bid = jnp.zeros_like(idx)
    for q in range(1, LRU_BLOCKS):
        bid = bid + (idx >= q * LRU_BW).astype(jnp.int32)
    return bid


def _gate_weights_kernel(wa_in_ref, wx_in_ref, wa_ref, wx_ref):
    col = lax.broadcasted_iota(jnp.int32, (1, D_RNN), 1)
    col_blk = _block_id(col)
    col_in_blk = col - LRU_BW * col_blk
    sel = (lax.broadcasted_iota(jnp.int32, (LRU_BW, D_RNN), 0) == col_in_blk).astype(BF16)
    row_blk = _block_id(lax.broadcasted_iota(jnp.int32, (D_RNN, 1), 0))
    mask = row_blk == col_blk
    for src, dst in ((wa_in_ref, wa_ref), (wx_in_ref, wx_ref)):
        z = jnp.dot(src[...].astype(BF16), sel, preferred_element_type=F32)
        dst[...] = jnp.where(mask, z, 0.0).astype(BF16)


def _gate_weights(w_a, w_x):
    dense = jax.ShapeDtypeStruct((D_RNN, D_RNN), BF16)
    return pl.pallas_call(
        _gate_weights_kernel,
        out_shape=[dense, dense],
        compiler_params=pltpu.CompilerParams(vmem_limit_bytes=VMEM_LIMIT_BYTES),
        name="gate_weights",
    )(w_a.reshape(D_RNN, LRU_BW), w_x.reshape(D_RNN, LRU_BW))


def _s5_kernel(x_ref, hre0_ref, him0_ref, g0_ref, g1_ref, are_ref, aim_ref, wb_ref, wc_ref,
               dskip_ref, w_out_ref,
               out_ref, hre_out_ref, him_out_ref,
               hre_ref, him_ref, bu_ref, y_ref, *, bsz, steps):
    rows = bsz * steps
    ts = S5_TILE_STATES

    @pl.when(pl.program_id(0) == 0)
    def _():
        hre_ref[...] = hre0_ref[...]
        him_ref[...] = him0_ref[...]

    x = x_ref[...]
    u = _rms(x, g0_ref[...])
    ub = u.astype(BF16)

    for k in range(S5_KTILES):
        bu = bu_ref.at[k % 2]
        bu[...] = jnp.dot(ub[:, k * MXU_DIM:(k + 1) * MXU_DIM], wb_ref[k], preferred_element_type=F32)
        for c0 in range(0, ts, S5_SCAN_W):
            lanes = slice(k * ts + c0, k * ts + c0 + S5_SCAN_W)
            re_cols = slice(c0, c0 + S5_SCAN_W)
            im_cols = slice(ts + c0, ts + c0 + S5_SCAN_W)
            ar = jnp.broadcast_to(are_ref[:, lanes], (bsz, S5_SCAN_W))
            ai = jnp.broadcast_to(aim_ref[:, lanes], (bsz, S5_SCAN_W))
            hr = hre_ref[:, lanes]
            hi = him_ref[:, lanes]
            for t in range(steps):
                rs = slice(t * bsz, (t + 1) * bsz)
                hr, hi = ((ar * hr - ai * hi) + bu[rs, re_cols],
                          (ar * hi + ai * hr) + bu[rs, im_cols])
                bu[rs, re_cols] = hr
                bu[rs, im_cols] = hi
            hre_ref[:, lanes] = hr
            him_ref[:, lanes] = hi
        y_ref[:, k * MXU_DIM:(k + 1) * MXU_DIM] = jnp.dot(
            bu[...].astype(BF16), wc_ref[k], preferred_element_type=F32)

    hre_out_ref[...] = hre_ref[...]
    him_out_ref[...] = him_ref[...]

    y = y_ref[...] + dskip_ref[...] * u
    z = jax.nn.gelu(y).astype(BF16)
    zz = jnp.dot(z, w_out_ref[...], preferred_element_type=F32)
    m = zz[:, :D_MODEL] * jax.nn.sigmoid(zz[:, D_MODEL:])
    out_ref[...] = x + _rms(m, g1_ref[...])


def _s5_layer(x, hre0, him0, g0, g1, are, aim, wb, wc, dskip, w_out, *, bsz, steps):
    n_rows = x.shape[0]
    rows = bsz * steps
    n_state = S5_GROUPS * S5_STATE
    assert n_rows % rows == 0
    kern = functools.partial(_s5_kernel, bsz=bsz, steps=steps)
    state = jax.ShapeDtypeStruct((bsz, n_state), F32)
    return pl.pallas_call(
        kern,
        grid=(n_rows // rows,),
        in_specs=[
            _row_spec(rows, D_MODEL),
            _const_spec((bsz, n_state)), _const_spec((bsz, n_state)),
            _const_spec((1, D_MODEL)), _const_spec((1, D_MODEL)),
            _const_spec((1, n_state)), _const_spec((1, n_state)),
            _const_spec((S5_KTILES, MXU_DIM, 2 * S5_TILE_STATES)),
            _const_spec((S5_KTILES, 2 * S5_TILE_STATES, MXU_DIM)),
            _const_spec((1, D_MODEL)),
            _const_spec((D_MODEL, 2 * D_MODEL)),
        ],
        out_specs=[
            _row_spec(rows, D_MODEL),
            pl.BlockSpec((bsz, n_state), lambda i: (0, 0)),
            pl.BlockSpec((bsz, n_state), lambda i: (0, 0)),
        ],
        out_shape=[jax.ShapeDtypeStruct((n_rows, D_MODEL), F32), state, state],
        scratch_shapes=[
            pltpu.VMEM((bsz, n_state), F32),
            pltpu.VMEM((bsz, n_state), F32),
            pltpu.VMEM((2, rows, 2 * S5_TILE_STATES), F32),
            pltpu.VMEM((rows, D_MODEL), F32),
        ],
        compiler_params=_params("arbitrary"),
        name="s5_layer",
    )(x, hre0, him0, g0, g1, are, aim, wb, wc, dskip, w_out)


def _row(v):
    return v.reshape(1, -1)


def kernel(x_prompt, x_sample, state_lru_h, state_lru_conv, state_s5_re, state_s5_im, meta_tokens, norm_gains, lru_w_in, lru_w_conv, lru_b_conv, lru_w_a, lru_b_a, lru_w_x, lru_b_x, lru_lambda, lru_w_out, s5_lambda_re, s5_lambda_im, s5_log_dt, s5_b_re, s5_b_im, s5_c_re, s5_c_im, s5_d, s5_w_out, ffn_w_in, ffn_w_out):
    bp, seq, _ = x_prompt.shape
    bs = x_sample.shape[0]
    assert seq % PROMPT_STEPS == 0
    n_state = S5_GROUPS * S5_STATE

    lru_in = lru_w_in[0].astype(BF16)
    wa, wx = _gate_weights(lru_w_a[0], lru_w_x[0])
    lru_out = lru_w_out[0].astype(BF16)
    ffn_in = ffn_w_in.astype(BF16)
    ffn_out = ffn_w_out.astype(BF16)
    s5_out = s5_w_out[0].astype(BF16)

    gc = S5_GROUPS * S5_GROUP
    bt_re = s5_b_re[0].transpose(0, 2, 1).reshape(gc, S5_STATE)
    bt_im = s5_b_im[0].transpose(0, 2, 1).reshape(gc, S5_STATE)
    are, aim, wb, wc = _s5_weights(
        s5_lambda_re[0], s5_lambda_im[0], s5_log_dt[0].reshape(S5_GROUPS, 1), bt_re, bt_im,
        s5_c_re[0].reshape(gc, S5_STATE), s5_c_im[0].reshape(gc, S5_STATE))
    are = are.reshape(1, n_state)
    aim = aim.reshape(1, n_state)

    g = norm_gains

    def run(x_in, h0, conv0, hre0, him0, bsz, steps, final_ffn=True, out_bsz=0):
        rows = bsz * steps
        x1, h_last, conv_new = _lru_layer(
            x_in, h0, conv0, _row(g[0, 0]), _row(g[0, 1]), lru_in, lru_w_conv[0], _row(lru_b_conv[0]),
            wa, _row(lru_b_a[0]), wx, _row(lru_b_x[0]), _row(lru_lambda[0]), lru_out,
            bsz=bsz, steps=steps)
        x2 = _ffn_layer(x1, _row(g[0, 2]), _row(g[0, 3]), ffn_in, ffn_out, layer=0, rows=rows)
        x3, hre, him = _s5_layer(
            x2, hre0, him0, _row(g[1, 0]), _row(g[1, 1]), are, aim, wb, wc, _row(s5_d[0]), s5_out,
            bsz=bsz, steps=steps)
        x4 = None
        if final_ffn:
            x4 = _ffn_layer(x3, _row(g[1, 2]), _row(g[1, 3]), ffn_in, ffn_out, layer=1, rows=rows,
                            out_bsz=out_bsz)
        return x4, h_last, conv_new, hre, him

    meta_rows = jnp.repeat(meta_tokens.astype(x_prompt.dtype), bp, axis=0)
    zeros_h = jnp.zeros((bp, D_RNN), F32)
    zeros_c = jnp.zeros(((CONV_W - 1) * bp, D_RNN), F32)
    zeros_s = jnp.zeros((bp, n_state), F32)
    _, hm, cm, rem, imm = run(meta_rows, zeros_h, zeros_c, zeros_s, zeros_s, bp, N_META, final_ffn=False)
    y_prompt, hp, cp, rep, imp = run(x_prompt, hm, cm, rem, imm, bp, PROMPT_STEPS, out_bsz=bp)

    xs = x_sample.reshape(bs, D_MODEL)
    conv0_s = state_lru_conv[0].transpose(1, 0, 2).reshape((CONV_W - 1) * bs, D_RNN)
    ys, hs, cs, res, ims = run(xs, state_lru_h[0], conv0_s,
                               state_s5_re[0].reshape(bs, n_state), state_s5_im[0].reshape(bs, n_state),
                               bs, 1)
    y_sample = ys.reshape(bs, 1, D_MODEL)

    def conv_out(c, b):
        return c.reshape(CONV_W - 1, b, D_RNN).transpose(1, 0, 2)[None]

    def s5_out_state(s, b):
        return s.reshape(1, b, S5_GROUPS, S5_STATE)

    return (y_prompt, y_sample,
            hp[None], conv_out(cp, bp), s5_out_state(rep, bp), s5_out_state(imp, bp),
            hs[None], conv_out(cs, bs), s5_out_state(res, bs), s5_out_state(ims, bs))
```
